```python
import functools
import jax, jax.numpy as jnp
from jax import lax
import numpy as np

D_MODEL = 1024
BATCH = 4
SEQ = 4096
DEPTH = 2
DEC_BATCH = 128
DEC_SEQ = 8
PAST_LEN = 2048
PAGE_SIZE = 128

N_HEADS_A = 8
HEAD_DIM_A = 64
WIDTH_A = N_HEADS_A * HEAD_DIM_A
Q_BLOCK = 128
WIDTH_B = 512
N_BLOCKS_B = 8
BLOCK_B = WIDTH_B // N_BLOCKS_B
CONV_B = 4
LRU_C = 8.0
WIDTH_C = 512
CONV_C = 3
N_BRANCH = 3
FFN_HIDDEN = 2816
RMS_EPS = 1e-6
IN_SIZES = (WIDTH_A, WIDTH_A, WIDTH_A, N_HEADS_A, WIDTH_B, WIDTH_B, WIDTH_C, WIDTH_C, WIDTH_C, N_BRANCH * D_MODEL)
N_IN = 3 * WIDTH_A + N_HEADS_A + 2 * WIDTH_B + 3 * WIDTH_C + N_BRANCH * D_MODEL
N_MIX = WIDTH_A + WIDTH_B + WIDTH_C

kernel_name = 'fox_rglru_shortconv_macaron_step'


def rms_norm(x, g):
    xf = x.astype(jnp.float32)
    y = xf * lax.rsqrt(jnp.mean(xf * xf, axis=-1, keepdims=True) + RMS_EPS)
    return (y * g.astype(jnp.float32)).astype(x.dtype)


def swiglu(x, w_in, w_out):
    gate, up = jnp.split(x @ w_in, 2, axis=-1)
    return (jax.nn.silu(gate) * up) @ w_out


def split_columns(z):
    parts = []
    start = 0
    for size in IN_SIZES:
        parts.append(z[..., start:start + size])
        start += size
    return parts


def fox_attend(q, k, v, cq, ck, q_pos, k_pos):
    s = jnp.einsum('nqhd,nkhd->nhqk', q, k).astype(jnp.float32) * (HEAD_DIM_A ** -0.5)
    s = s + jnp.transpose(cq, (0, 2, 1))[:, :, :, None] - jnp.transpose(ck, (0, 2, 1))[:, :, None, :]
    mask = k_pos[None, :] <= q_pos[:, None]
    s = jnp.where(mask, s, -jnp.inf)
    p = jax.nn.softmax(s, axis=-1)
    return jnp.einsum('nhqk,nkhd->nqhd', p.astype(v.dtype), v)


def fox_prompt(q, k, v, logf):
    n, t, h, d = q.shape
    c = jnp.cumsum(logf, axis=1)
    nb = t // Q_BLOCK
    k_pos = jnp.arange(t)
    qb = jnp.moveaxis(q.reshape(n, nb, Q_BLOCK, h, d), 1, 0)
    cqb = jnp.moveaxis(c.reshape(n, nb, Q_BLOCK, h), 1, 0)
    starts = jnp.arange(nb) * Q_BLOCK

    def block(args):
        qi, cqi, s0 = args
        return fox_attend(qi, k, v, cqi, c, s0 + jnp.arange(Q_BLOCK), k_pos)

    o = lax.map(block, (qb, cqb, starts))
    return jnp.moveaxis(o, 0, 1).reshape(n, t, h * d)


def fox_sample(q, k, v, logf, past_k, past_v, past_logf):
    n, t, h, d = q.shape
    p = past_k.shape[1]
    k_all = jnp.concatenate([past_k.astype(k.dtype), k], axis=1)
    v_all = jnp.concatenate([past_v.astype(v.dtype), v], axis=1)
    c = jnp.cumsum(jnp.concatenate([past_logf.astype(jnp.float32), logf], axis=1), axis=1)
    o = fox_attend(q, k_all, v_all, c[:, p:], c, p + jnp.arange(t), jnp.arange(p + t))
    return o.reshape(n, t, h * d)


def causal_dwconv(u, buf, w):
    kw = w.shape[0]
    t = u.shape[1]
    up = jnp.concatenate([buf.astype(u.dtype), u], axis=1)
    out = w[0] * up[:, 0:t]
    for j in range(1, kw):
        out = out + w[j] * up[:, j:j + t]
    return out, up[:, up.shape[1] - (kw - 1):]


def rg_lru(xc, h0, wa, ba, wx, bx, lam):
    n, t, w = xc.shape
    xb = xc.reshape(n, t, N_BLOCKS_B, BLOCK_B)
    r = jax.nn.sigmoid(jnp.einsum('ntgi,gij->ntgj', xb, wa).reshape(n, t, w) + ba).astype(jnp.float32)
    i = jax.nn.sigmoid(jnp.einsum('ntgi,gij->ntgj', xb, wx).reshape(n, t, w) + bx).astype(jnp.float32)
    log_a = -LRU_C * r * jax.nn.softplus(-lam.astype(jnp.float32))
    a = jnp.exp(log_a)
    b = jnp.sqrt(-jnp.expm1(2.0 * log_a)) * (i * xc.astype(jnp.float32))
    b = b.at[:, 0].add(a[:, 0] * h0.astype(jnp.float32))

    def combine(left, right):
        a1, b1 = left
        a2, b2 = right
        return a1 * a2, a2 * b1 + b2

    _, h = lax.associative_scan(combine, (a, b), axis=1)
    return h.astype(xc.dtype), h[:, -1].astype(xc.dtype)


def hybrid_layer(x, p, attn_fn, lru_h0, lru_buf, sc_buf):
    x = x + 0.5 * swiglu(rms_norm(x, p['ffn1_norm']), p['ffn1_w_in'], p['ffn1_w_out'])
    h = rms_norm(x, p['mix_norm'])
    n, t, _ = h.shape
    q, k, v, f, bx_in, by_in, sb, sc, sx, g = split_columns(h @ p['w_in'])
    q = rms_norm(q.reshape(n, t, N_HEADS_A, HEAD_DIM_A), p['q_norm'])
    k = rms_norm(k.reshape(n, t, N_HEADS_A, HEAD_DIM_A), p['k_norm'])
    v = v.reshape(n, t, N_HEADS_A, HEAD_DIM_A)
    logf = jax.nn.log_sigmoid((f + p['b_forget']).astype(jnp.float32))
    o_a = attn_fn(q, k, v, logf)
    xc, lru_buf_new = causal_dwconv(bx_in, lru_buf, p['lru_conv_w'])
    xc = xc + p['lru_conv_b']
    hb, h_last = rg_lru(xc, lru_h0, p['lru_wa'], p['lru_ba'], p['lru_wx'], p['lru_bx'], p['lru_lambda'])
    o_b = hb * jax.nn.gelu(by_in)
    uc, sc_buf_new = causal_dwconv(sc * sx, sc_buf, p['sc_conv_w'])
    o_c = sb * uc
    wb = p['w_branch']
    gates = jax.nn.sigmoid(g.reshape(n, t, N_BRANCH, D_MODEL))
    merged = (gates[:, :, 0] * (o_a @ wb[:WIDTH_A])
              + gates[:, :, 1] * (o_b @ wb[WIDTH_A:WIDTH_A + WIDTH_B])
              + gates[:, :, 2] * (o_c @ wb[WIDTH_A + WIDTH_B:]))
    x = x + merged @ p['w_out']
    x = x + 0.5 * swiglu(rms_norm(x, p['ffn2_norm']), p['ffn2_w_in'], p['ffn2_w_out'])
    return x, (k, v, logf, h_last, lru_buf_new, sc_buf_new)


def setup_inputs(seed: int = 0) -> dict:
    key = jax.random.key(seed)
    ks = iter(jax.random.split(key, 48))

    def nrm(shape, scale):
        return jax.random.normal(next(ks), shape, jnp.float32) * scale

    def gain(shape):
        return 1.0 + nrm(shape, 0.02)

    n_pages = PAST_LEN // PAGE_SIZE
    n_used = DEC_BATCH * n_pages
    n_phys = n_used + n_used // 4
    page_table = jax.random.permutation(next(ks), n_phys)[:n_used].reshape(DEC_BATCH, n_pages).astype(jnp.int32)
    u = jax.random.uniform(next(ks), (DEPTH, WIDTH_B), jnp.float32, minval=0.9, maxval=0.999)
    a0 = u ** (1.0 / LRU_C)
    lru_lambda = jnp.log(a0) - jnp.log1p(-a0)
    b_forget = jax.random.uniform(next(ks), (DEPTH, N_HEADS_A), jnp.float32, minval=1.0, maxval=6.0)
    return {
        'x_prompt': nrm((BATCH, SEQ, D_MODEL), 1.0),
        'x_sample': nrm((DEC_BATCH, DEC_SEQ, D_MODEL), 1.0),
        'cache_k': nrm((n_phys, DEPTH, PAGE_SIZE, N_HEADS_A, HEAD_DIM_A), 1.0),
        'cache_v': nrm((n_phys, DEPTH, PAGE_SIZE, N_HEADS_A, HEAD_DIM_A), 1.0),
        'cache_logf': jax.nn.log_sigmoid(nrm((n_phys, DEPTH, PAGE_SIZE, N_HEADS_A), 1.0) + 4.0),
        'state_lru_h': nrm((DEC_BATCH, DEPTH, WIDTH_B), 0.5),
        'state_lru_conv': nrm((DEC_BATCH, DEPTH, CONV_B - 1, WIDTH_B), 1.0),
        'state_sc_conv': nrm((DEC_BATCH, DEPTH, CONV_C - 1, WIDTH_C), 1.0),
        'page_table': page_table,
        'ffn1_norm': gain((DEPTH, D_MODEL)),
        'ffn1_w_in': nrm((DEPTH, D_MODEL, 2 * FFN_HIDDEN), D_MODEL ** -0.5),
        'ffn1_w_out': nrm((DEPTH, FFN_HIDDEN, D_MODEL), FFN_HIDDEN ** -0.5),
        'mix_norm': gain((DEPTH, D_MODEL)),
        'w_in': nrm((DEPTH, D_MODEL, N_IN), D_MODEL ** -0.5),
        'b_forget': b_forget,
        'q_norm': gain((DEPTH, HEAD_DIM_A)),
        'k_norm': gain((DEPTH, HEAD_DIM_A)),
        'lru_conv_w': nrm((DEPTH, CONV_B, WIDTH_B), CONV_B ** -0.5),
        'lru_conv_b': nrm((DEPTH, WIDTH_B), 0.01),
        'lru_wa': nrm((DEPTH, N_BLOCKS_B, BLOCK_B, BLOCK_B), BLOCK_B ** -0.5),
        'lru_ba': nrm((DEPTH, WIDTH_B), 0.01),
        'lru_wx': nrm((DEPTH, N_BLOCKS_B, BLOCK_B, BLOCK_B), BLOCK_B ** -0.5),
        'lru_bx': nrm((DEPTH, WIDTH_B), 0.01),
        'lru_lambda': lru_lambda,
        'sc_conv_w': nrm((DEPTH, CONV_C, WIDTH_C), CONV_C ** -0.5),
        'w_branch': nrm((DEPTH, N_MIX, D_MODEL), WIDTH_A ** -0.5),
        'w_out': nrm((DEPTH, D_MODEL, D_MODEL), D_MODEL ** -0.5),
        'ffn2_norm': gain((DEPTH, D_MODEL)),
        'ffn2_w_in': nrm((DEPTH, D_MODEL, 2 * FFN_HIDDEN), D_MODEL ** -0.5),
        'ffn2_w_out': nrm((DEPTH, FFN_HIDDEN, D_MODEL), FFN_HIDDEN ** -0.5),
    }


def reference(x_prompt, x_sample, cache_k, cache_v, cache_logf, state_lru_h, state_lru_conv, state_sc_conv,
              page_table, ffn1_norm, ffn1_w_in, ffn1_w_out, mix_norm, w_in, b_forget, q_norm, k_norm,
              lru_conv_w, lru_conv_b, lru_wa, lru_ba, lru_wx, lru_bx, lru_lambda, sc_conv_w, w_branch, w_out,
              ffn2_norm, ffn2_w_in, ffn2_w_out):
    n_prompt = x_prompt.shape[0]
    n_dec = page_table.shape[0]
    zero_h = jnp.zeros((n_prompt, WIDTH_B), x_prompt.dtype)
    zero_lru_buf = jnp.zeros((n_prompt, CONV_B - 1, WIDTH_B), x_prompt.dtype)
    zero_sc_buf = jnp.zeros((n_prompt, CONV_C - 1, WIDTH_C), x_prompt.dtype)
    yp = x_prompt
    ys = x_sample
    pk, pv, plf, ph, pcb, psb = [], [], [], [], [], []
    sk, sv, slf, sh, scb, ssb = [], [], [], [], [], []
    for l in range(DEPTH):
        p = {
            'ffn1_norm': ffn1_norm[l], 'ffn1_w_in': ffn1_w_in[l], 'ffn1_w_out': ffn1_w_out[l],
            'mix_norm': mix_norm[l], 'w_in': w_in[l], 'b_forget': b_forget[l],
            'q_norm': q_norm[l], 'k_norm': k_norm[l],
            'lru_conv_w': lru_conv_w[l], 'lru_conv_b': lru_conv_b[l],
            'lru_wa': lru_wa[l], 'lru_ba': lru_ba[l], 'lru_wx': lru_wx[l], 'lru_bx': lru_bx[l],
            'lru_lambda': lru_lambda[l], 'sc_conv_w': sc_conv_w[l],
            'w_branch': w_branch[l], 'w_out': w_out[l],
            'ffn2_norm': ffn2_norm[l], 'ffn2_w_in': ffn2_w_in[l], 'ffn2_w_out': ffn2_w_out[l],
        }
        yp, st = hybrid_layer(yp, p, fox_prompt, zero_h, zero_lru_buf, zero_sc_buf)
        pk.append(st[0]); pv.append(st[1]); plf.append(st[2]); ph.append(st[3]); pcb.append(st[4]); psb.append(st[5])
        past_k = cache_k[page_table, l].reshape(n_dec, -1, N_HEADS_A, HEAD_DIM_A)
        past_v = cache_v[page_table, l].reshape(n_dec, -1, N_HEADS_A, HEAD_DIM_A)
        past_lf = cache_logf[page_table, l].reshape(n_dec, -1, N_HEADS_A)
        attn_s = functools.partial(fox_sample, past_k=past_k, past_v=past_v, past_logf=past_lf)
        ys, st = hybrid_layer(ys, p, attn_s, state_lru_h[:, l], state_lru_conv[:, l], state_sc_conv[:, l])
        sk.append(st[0]); sv.append(st[1]); slf.append(st[2]); sh.append(st[3]); scb.append(st[4]); ssb.append(st[5])
    prompt_k = jnp.stack(pk, axis=1)
    prompt_v = jnp.stack(pv, axis=1)
    prompt_logf = jnp.stack(plf, axis=1)
    prompt_lru_h = jnp.stack(ph, axis=1)
    prompt_lru_conv = jnp.stack(pcb, axis=1)
    prompt_sc_conv = jnp.stack(psb, axis=1)
    sample_k = jnp.stack(sk, axis=1)
    sample_v = jnp.stack(sv, axis=1)
    sample_logf = jnp.stack(slf, axis=1)
    sample_lru_h = jnp.stack(sh, axis=1)
    sample_lru_conv = jnp.stack(scb, axis=1)
    sample_sc_conv = jnp.stack(ssb, axis=1)
    return (yp, ys, prompt_k, prompt_v, prompt_logf, prompt_lru_h, prompt_lru_conv, prompt_sc_conv,
            sample_k, sample_v, sample_logf, sample_lru_h, sample_lru_conv, sample_sc_conv)
```

```python
import functools

import jax
import jax.numpy as jnp
from jax import lax
from jax.experimental import pallas as pl
from jax.experimental.pallas import tpu as pltpu

F32 = jnp.float32
BF16 = jnp.bfloat16

D_MODEL = 1024
N_HEADS = 8
HEAD_DIM = 64
WIDTH_A = N_HEADS * HEAD_DIM
WIDTH_B = 512
N_BLOCKS_B = 8
CONV_B = 4
LRU_C = 8.0
WIDTH_C = 512
CONV_C = 3
N_BRANCH = 3
FFN_HIDDEN = 2816
RMS_EPS = 1e-6
PAGE = 128

LANES = 128
SUBLANES = 8
VMEM_LIMIT = 56 * 1024 * 1024

OFF_F = 3 * WIDTH_A
OFF_MIX = OFF_F + N_HEADS
W_MIX = 2 * WIDTH_B + 3 * WIDTH_C
OFF_G = OFF_MIX + W_MIX
W_G = N_BRANCH * D_MODEL
P_MIX = 3 * WIDTH_A
P_G = P_MIX + W_MIX
P_F = P_G + W_G
P_END = P_F + LANES

NEG = -1e30
ROW_TILE = 512
FFN_CHUNK = 1408
ATT_TILE = 512
MIX_TILE = 512
SEQ_PER_MIX_STEP = 64


def _params(sem):
    return pltpu.CompilerParams(dimension_semantics=sem, vmem_limit_bytes=VMEM_LIMIT)


def _resident(shape, index_map):
    return pl.BlockSpec(shape, index_map, pipeline_mode=pl.Buffered(1))


def _rms(x, g):
    return (x * lax.rsqrt(jnp.mean(x * x, axis=-1, keepdims=True) + RMS_EPS)) * g


def _softplus(x):
    return jnp.maximum(x, 0.0) + jnp.log1p(jnp.exp(-jnp.abs(x)))


def _dot(a, b):
    return jnp.dot(a, b, preferred_element_type=F32)


def _dot_nt(a, b):
    return lax.dot_general(a, b, (((1,), (1,)), ((), ())), preferred_element_type=F32)


def _ffn_kernel(x_ref, g_ref, w1_ref, w2_ref, o_ref):
    x = x_ref[...]
    xn = _rms(x, g_ref[...]).astype(BF16)
    acc = jnp.zeros(x.shape, F32)
    for c in range(FFN_HIDDEN // FFN_CHUNK):
        lo = c * FFN_CHUNK
        gate = _dot(xn, w1_ref[:, lo:lo + FFN_CHUNK])
        up = _dot(xn, w1_ref[:, FFN_HIDDEN + lo:FFN_HIDDEN + lo + FFN_CHUNK])
        act = (gate * jax.nn.sigmoid(gate)) * up
        acc = acc + _dot(act.astype(BF16), w2_ref[lo:lo + FFN_CHUNK, :])
    o_ref[...] = x + 0.5 * acc


def _ffn(x, norm, w1, w2, layer):
    rows = x.shape[0]
    return pl.pallas_call(
        _ffn_kernel,
        grid=(rows // ROW_TILE,),
        in_specs=[
            pl.BlockSpec((ROW_TILE, D_MODEL), lambda i: (i, 0)),
            pl.BlockSpec((None, 1, D_MODEL), lambda i: (layer, 0, 0)),
            _resident((None, D_MODEL, 2 * FFN_HIDDEN), lambda i: (layer, 0, 0)),
            _resident((None, FFN_HIDDEN, D_MODEL), lambda i: (layer, 0, 0)),
        ],
        out_specs=pl.BlockSpec((ROW_TILE, D_MODEL), lambda i: (i, 0)),
        out_shape=jax.ShapeDtypeStruct((rows, D_MODEL), F32),
        compiler_params=_params(("parallel",)),
        name="ffn",
    )(x, norm, w1, w2)


def _head_rms(x, gamma):
    m = x.shape[0]
    lane = lax.broadcasted_iota(jnp.int32, (m, LANES), 1)
    first = lane < HEAD_DIM
    outs = []
    for c in range(WIDTH_A // LANES):
        xc = x[:, c * LANES:(c + 1) * LANES]
        sq = xc * xc
        s_lo = jnp.sum(jnp.where(first, sq, 0.0), axis=-1, keepdims=True)
        s_hi = jnp.sum(jnp.where(first, 0.0, sq), axis=-1, keepdims=True)
        ms = jnp.where(first, s_lo, s_hi) * (1.0 / HEAD_DIM)
        outs.append((xc * lax.rsqrt(ms + RMS_EPS)) * gamma[:, c * LANES:(c + 1) * LANES])
    return jnp.concatenate(outs, axis=-1)


def _inproj_kernel(x_ref, g_ref, w_ref, bf_ref, qn_ref, kn_ref,
                   q_ref, k_ref, v_ref, qb_ref, kb_ref, vb_ref, lf_ref, lft_ref, mix_ref, gate_ref):
    xn = _rms(x_ref[...], g_ref[...]).astype(BF16)
    qkv = _dot(xn, w_ref[:, 0:P_MIX])
    q = _head_rms(qkv[:, 0:WIDTH_A], qn_ref[...]) * (HEAD_DIM ** -0.5)
    k = _head_rms(qkv[:, WIDTH_A:2 * WIDTH_A], kn_ref[...])
    v = qkv[:, 2 * WIDTH_A:3 * WIDTH_A]
    q_ref[...] = q
    k_ref[...] = k
    v_ref[...] = v
    qb_ref[...] = q.astype(BF16)
    kb_ref[...] = k.astype(BF16)
    vb_ref[...] = v.astype(BF16)
    mix_ref[...] = _dot(xn, w_ref[:, P_MIX:P_G])
    gate_ref[...] = _dot(xn, w_ref[:, P_G:P_F])
    f = _dot(xn, w_ref[:, P_F:P_END]) + bf_ref[...]
    lf = jnp.minimum(f, 0.0) - jnp.log1p(jnp.exp(-jnp.abs(f)))
    lf_ref[...] = lf
    lft_ref[...] = lf.T[0:N_HEADS, :]


def _inproj(x, norm, w, bf, qn, kn, layer):
    rows = x.shape[0]
    row_spec = lambda w_: pl.BlockSpec((ROW_TILE, w_), lambda i: (i, 0))
    vec_spec = lambda w_: pl.BlockSpec((None, 1, w_), lambda i: (layer, 0, 0))
    return pl.pallas_call(
        _inproj_kernel,
        grid=(rows // ROW_TILE,),
        in_specs=[
            row_spec(D_MODEL),
            vec_spec(D_MODEL),
            _resident((None, D_MODEL, P_END), lambda i: (layer, 0, 0)),
            vec_spec(LANES),
            vec_spec(WIDTH_A),
            vec_spec(WIDTH_A),
        ],
        out_specs=[
            row_spec(WIDTH_A), row_spec(WIDTH_A), row_spec(WIDTH_A),
            row_spec(WIDTH_A), row_spec(WIDTH_A), row_spec(WIDTH_A),
            row_spec(LANES),
            pl.BlockSpec((N_HEADS, ROW_TILE), lambda i: (0, i)),
            row_spec(W_MIX),
            row_spec(W_G),
        ],
        out_shape=[
            jax.ShapeDtypeStruct((rows, WIDTH_A), F32),
            jax.ShapeDtypeStruct((rows, WIDTH_A), F32),
            jax.ShapeDtypeStruct((rows, WIDTH_A), F32),
            jax.ShapeDtypeStruct((rows, WIDTH_A), BF16),
            jax.ShapeDtypeStruct((rows, WIDTH_A), BF16),
            jax.ShapeDtypeStruct((rows, WIDTH_A), BF16),
            jax.ShapeDtypeStruct((rows, LANES), F32),
            jax.ShapeDtypeStruct((N_HEADS, rows), F32),
            jax.ShapeDtypeStruct((rows, W_MIX), F32),
            jax.ShapeDtypeStruct((rows, W_G), F32),
        ],
        compiler_params=_params(("parallel",)),
        name="inproj",
    )(x, norm, w, bf, qn, kn)


def _lane_prefix(x, n_valid):
    lane = lax.broadcasted_iota(jnp.int32, x.shape, 1)
    shift = 1
    while shift < n_valid:
        x = x + jnp.where(lane >= shift, pltpu.roll(x, shift, 1), 0.0)
        shift *= 2
    return x


def _cumsum_kernel(x_ref, o_ref):
    o_ref[...] = _lane_prefix(x_ref[...], x_ref.shape[1])


def _cumsum(lft, batch, seq):
    return pl.pallas_call(
        _cumsum_kernel,
        grid=(batch,),
        in_specs=[pl.BlockSpec((N_HEADS, seq), lambda b: (0, b))],
        out_specs=pl.BlockSpec((N_HEADS, seq), lambda b: (0, b)),
        out_shape=jax.ShapeDtypeStruct((N_HEADS, batch * seq), F32),
        compiler_params=_params(("parallel",)),
        name="cumsum",
    )(lft)


def _attn_p_kernel(q_ref, k_ref, v_ref, c_ref, ct_ref, o_ref, m_ref, l_ref, acc_ref):
    t = ATT_TILE
    hp = pl.program_id(1)
    qi = pl.program_id(2)
    q2 = q_ref[...]
    lane = lax.broadcasted_iota(jnp.int32, (t, LANES), 1)
    first = lane < HEAD_DIM
    zero = jnp.zeros_like(q2)
    q_heads = (jnp.where(first, q2, zero), jnp.where(first, zero, q2))
    crow = c_ref[...]
    hlane = lax.broadcasted_iota(jnp.int32, crow.shape, 1)
    cq = [jnp.sum(jnp.where(hlane == 2 * hp + e, crow, 0.0), axis=-1, keepdims=True) for e in range(2)]
    m_ref[...] = jnp.full(m_ref.shape, NEG, F32)
    l_ref[...] = jnp.zeros(l_ref.shape, F32)
    acc_ref[...] = jnp.zeros(acc_ref.shape, F32)
    row = lax.broadcasted_iota(jnp.int32, (t, t), 0)
    col = lax.broadcasted_iota(jnp.int32, (t, t), 1)
    causal = col <= row

    def step(j, masked):
        start = pl.multiple_of(j * t, t)
        kc = k_ref[pl.ds(start, t), :]
        vc = v_ref[pl.ds(start, t), :]
        for e in range(2):
            ck = ct_ref[pl.ds(2 * hp + e, 1), pl.ds(start, t)]
            s = (_dot_nt(q_heads[e], kc) + cq[e]) - ck
            if masked:
                s = jnp.where(causal, s, NEG)
            m_prev = m_ref[e]
            m_new = jnp.maximum(m_prev, jnp.max(s, axis=-1, keepdims=True))
            alpha = jnp.exp(m_prev - m_new)
            p = jnp.exp(s - m_new)
            l_ref[e] = alpha * l_ref[e] + jnp.sum(p, axis=-1, keepdims=True)
            acc_ref[e] = alpha * acc_ref[e] + _dot(p.astype(BF16), vc)
            m_ref[e] = m_new

    def body(j, carry):
        step(j, False)
        return carry

    lax.fori_loop(0, qi, body, 0)
    step(qi, True)
    o_ref[...] = jnp.where(first, acc_ref[0] / l_ref[0], acc_ref[1] / l_ref[1])


def _attn_p(qb, kb, vb, c_rows, ct, batch, seq):
    t = ATT_TILE
    nq = seq // t
    return pl.pallas_call(
        _attn_p_kernel,
        grid=(batch, WIDTH_A // LANES, nq),
        in_specs=[
            pl.BlockSpec((t, LANES), lambda b, h, i: (b * nq + i, h)),
            pl.BlockSpec((seq, LANES), lambda b, h, i: (b, h)),
            pl.BlockSpec((seq, LANES), lambda b, h, i: (b, h)),
            pl.BlockSpec((t, N_HEADS), lambda b, h, i: (b * nq + i, 0)),
            pl.BlockSpec((N_HEADS, seq), lambda b, h, i: (0, b)),
        ],
        out_specs=pl.BlockSpec((t, LANES), lambda b, h, i: (b * nq + i, h)),
        out_shape=jax.ShapeDtypeStruct((batch * seq, WIDTH_A), F32),
        scratch_shapes=[
            pltpu.VMEM((2, t, 1), F32),
            pltpu.VMEM((2, t, 1), F32),
            pltpu.VMEM((2, t, LANES), F32),
        ],
        compiler_params=_params(("parallel", "parallel", "parallel")),
        name="attn_p",
    )(qb, kb, vb, c_rows, ct)


def _transpose_rows8(x):
    pad = jnp.concatenate([x, jnp.zeros((LANES - SUBLANES, LANES), F32)], axis=0)
    return pad.T[0:SUBLANES, :]


def _attn_s_kernel(pt_ref, q_ref, kn_ref, vn_ref, lfn_ref, *rest, n_pages, t_new):
    k_refs = rest[0:n_pages]
    v_refs = rest[n_pages:2 * n_pages]
    lf_refs = rest[2 * n_pages:3 * n_pages]
    o_ref = rest[3 * n_pages]
    rows = N_HEADS * t_new
    rhead = lax.broadcasted_iota(jnp.int32, (rows, WIDTH_A), 0) // t_new
    chead = lax.broadcasted_iota(jnp.int32, (rows, WIDTH_A), 1) // HEAD_DIM
    diag = rhead == chead
    q = q_ref[...]
    qbd = jnp.where(diag, jnp.concatenate([q] * N_HEADS, axis=0), 0.0).astype(BF16)

    def rep(ct):
        return jnp.broadcast_to(ct[:, None, :], (N_HEADS, t_new, LANES)).reshape(rows, LANES)

    carry = jnp.zeros((N_HEADS, 1), F32)
    cts = []
    for j in range(n_pages):
        ct = _lane_prefix(lf_refs[j][...], PAGE) + carry
        cts.append(ct)
        carry = ct[:, PAGE - 1:PAGE]
    lft_new = _transpose_rows8(lfn_ref[...])
    ct_new = _lane_prefix(lft_new, t_new) + carry
    c_new = _transpose_rows8(ct_new)
    hl = lax.broadcasted_iota(jnp.int32, (rows, LANES), 1)
    rh = lax.broadcasted_iota(jnp.int32, (rows, LANES), 0) // t_new
    cq = jnp.sum(jnp.where(hl == rh, jnp.concatenate([c_new] * N_HEADS, axis=0), 0.0),
                 axis=-1, keepdims=True)

    s_blocks = []
    for j in range(n_pages):
        s = _dot_nt(qbd, k_refs[j][...].astype(BF16))
        s_blocks.append((s + cq) - rep(cts[j]))
    zeros_pad = jnp.zeros((PAGE - t_new, WIDTH_A), F32)
    k_new = jnp.concatenate([kn_ref[...], zeros_pad], axis=0).astype(BF16)
    v_new = jnp.concatenate([vn_ref[...], zeros_pad], axis=0).astype(BF16)
    s = (_dot_nt(qbd, k_new) + cq) - rep(ct_new)
    tq = lax.broadcasted_iota(jnp.int32, (rows, LANES), 0) % t_new
    s_blocks.append(jnp.where(hl <= tq, s, NEG))

    m = s_blocks[0]
    for s in s_blocks[1:]:
        m = jnp.maximum(m, s)
    m = jnp.max(m, axis=-1, keepdims=True)
    l = jnp.zeros((rows, 1), F32)
    acc = jnp.zeros((rows, WIDTH_A), F32)
    for j in range(n_pages + 1):
        p = jnp.exp(s_blocks[j] - m)
        l = l + jnp.sum(p, axis=-1, keepdims=True)
        vj = v_new if j == n_pages else v_refs[j][...].astype(BF16)
        acc = acc + _dot(p.astype(BF16), vj)
    o = jnp.where(diag, acc / l, 0.0)
    out = o[0:t_new, :]
    for h in range(1, N_HEADS):
        out = out + o[h * t_new:(h + 1) * t_new, :]
    o_ref[...] = out


def _attn_s(page_table, q, k, v, lf, cache_k, cache_v, cache_lft, layer, row0, t_new):
    n_seq, n_pages = page_table.shape
    blk0 = row0 // t_new
    new_spec = lambda w_: pl.BlockSpec((t_new, w_), lambda n, pt: (blk0 + n, 0))

    def page_spec(j, r, w_):
        return pl.BlockSpec((None, None, r, w_), lambda n, pt: (pt[n, j], layer, 0, 0))

    in_specs = [new_spec(WIDTH_A), new_spec(WIDTH_A), new_spec(WIDTH_A), new_spec(LANES)]
    in_specs += [page_spec(j, PAGE, WIDTH_A) for j in range(n_pages)]
    in_specs += [page_spec(j, PAGE, WIDTH_A) for j in range(n_pages)]
    in_specs += [page_spec(j, N_HEADS, PAGE) for j in range(n_pages)]
    grid_spec = pltpu.PrefetchScalarGridSpec(
        num_scalar_prefetch=1,
        grid=(n_seq,),
        in_specs=in_specs,
        out_specs=pl.BlockSpec((t_new, WIDTH_A), lambda n, pt: (n, 0)),
    )
    return pl.pallas_call(
        functools.partial(_attn_s_kernel, n_pages=n_pages, t_new=t_new),
        grid_spec=grid_spec,
        out_shape=jax.ShapeDtypeStruct((n_seq * t_new, WIDTH_A), F32),
        compiler_params=_params(("parallel",)),
        name="attn_s",
    )(page_table, q, k, v, lf, *([cache_k] * n_pages), *([cache_v] * n_pages), *([cache_lft] * n_pages))


def _lru_coeffs(xc, wa_ref, ba_ref, wx_ref, bx_ref, lam_ref):
    xb = xc.astype(BF16)
    r = jax.nn.sigmoid(_dot(xb, wa_ref[...]) + ba_ref[...])
    ig = jax.nn.sigmoid(_dot(xb, wx_ref[...]) + bx_ref[...])
    log_a = (-LRU_C * r) * _softplus(-lam_ref[...])
    a = jnp.exp(log_a)
    b = jnp.sqrt(1.0 - a * a) * (ig * xc)
    return a, b


def _mix_p_kernel(z_ref, cw_ref, cb_ref, wa_ref, ba_ref, wx_ref, bx_ref, lam_ref, sw_ref,
                  ob_ref, oc_ref, h_ref, lc_ref, scc_ref,
                  ubuf, pbuf, a_s, b_s, hcar):
    tt = MIX_TILE
    hist = SUBLANES
    i = pl.program_id(1)

    @pl.when(i == 0)
    def _():
        ubuf[0:hist, :] = jnp.zeros((hist, WIDTH_B), F32)
        pbuf[0:hist, :] = jnp.zeros((hist, WIDTH_C), F32)
        hcar[...] = jnp.zeros(hcar.shape, F32)

    u = z_ref[:, 0:WIDTH_B]
    ubuf[hist:hist + tt, :] = u
    xc = cw_ref[0:1, :] * ubuf[hist - 3:hist - 3 + tt, :]
    xc = xc + cw_ref[1:2, :] * ubuf[hist - 2:hist - 2 + tt, :]
    xc = xc + cw_ref[2:3, :] * ubuf[hist - 1:hist - 1 + tt, :]
    xc = xc + cw_ref[3:4, :] * u
    xc = xc + cb_ref[...]
    ubuf[0:hist, :] = ubuf[tt:tt + hist, :]
    lc_ref[...] = z_ref[tt - (CONV_B - 1):tt, 0:WIDTH_B]

    a, b = _lru_coeffs(xc, wa_ref, ba_ref, wx_ref, bx_ref, lam_ref)
    a_s[...] = a
    b_s[...] = b
    srow = lax.broadcasted_iota(jnp.int32, (SUBLANES, WIDTH_B), 0)

    def chunk(c, h):
        r0 = pl.multiple_of(c * SUBLANES, SUBLANES)
        aa = a_s[pl.ds(r0, SUBLANES), :]
        bb = b_s[pl.ds(r0, SUBLANES), :]
        for s in (1, 2, 4):
            keep = srow >= s
            a_sh = jnp.where(keep, pltpu.roll(aa, s, 0), 1.0)
            b_sh = jnp.where(keep, pltpu.roll(bb, s, 0), 0.0)
            bb = aa * b_sh + bb
            aa = aa * a_sh
        hc = aa * h + bb
        b_s[pl.ds(r0, SUBLANES), :] = hc
        return hc[SUBLANES - 1:SUBLANES, :]

    h_last = lax.fori_loop(0, tt // SUBLANES, chunk, hcar[...], unroll=4)
    hcar[...] = h_last
    h_ref[...] = h_last
    ob_ref[...] = b_s[...] * jax.nn.gelu(z_ref[:, WIDTH_B:2 * WIDTH_B])

    o3 = 2 * WIDTH_B
    pc = z_ref[:, o3 + WIDTH_C:o3 + 2 * WIDTH_C] * z_ref[:, o3 + 2 * WIDTH_C:o3 + 3 * WIDTH_C]
    pbuf[hist:hist + tt, :] = pc
    uc = sw_ref[0:1, :] * pbuf[hist - 2:hist - 2 + tt, :]
    uc = uc + sw_ref[1:2, :] * pbuf[hist - 1:hist - 1 + tt, :]
    uc = uc + sw_ref[2:3, :] * pc
    pbuf[0:hist, :] = pbuf[tt:tt + hist, :]
    scc_ref[...] = pbuf[hist + tt - (CONV_C - 1):hist + tt, :]
    oc_ref[...] = z_ref[:, o3:o3 + WIDTH_C] * uc


def _mix_weight_specs(layer, n_grid):
    zeros = (0,) * 2

    def im(*_):
        return (layer,) + zeros

    return [
        pl.BlockSpec((None, CONV_B, WIDTH_B), im),
        pl.BlockSpec((None, 1, WIDTH_B), im),
        pl.BlockSpec((None, WIDTH_B, WIDTH_B), im),
        pl.BlockSpec((None, 1, WIDTH_B), im),
        pl.BlockSpec((None, WIDTH_B, WIDTH_B), im),
        pl.BlockSpec((None, 1, WIDTH_B), im),
        pl.BlockSpec((None, 1, WIDTH_B), im),
        pl.BlockSpec((None, CONV_C, WIDTH_C), im),
    ]


def _mix_p(zmix, weights, layer, batch, seq):
    tt = MIX_TILE
    nt = seq // tt
    rows = batch * seq
    return pl.pallas_call(
        _mix_p_kernel,
        grid=(batch, nt),
        in_specs=[pl.BlockSpec((tt, W_MIX), lambda b, i: (b * nt + i, 0))] + _mix_weight_specs(layer, 2),
        out_specs=[
            pl.BlockSpec((tt, WIDTH_B), lambda b, i: (b * nt + i, 0)),
            pl.BlockSpec((tt, WIDTH_C), lambda b, i: (b * nt + i, 0)),
            pl.BlockSpec((None, 1, WIDTH_B), lambda b, i: (b, 0, 0)),
            pl.BlockSpec((None, CONV_B - 1, WIDTH_B), lambda b, i: (b, 0, 0)),
            pl.BlockSpec((None, CONV_C - 1, WIDTH_C), lambda b, i: (b, 0, 0)),
        ],
        out_shape=[
            jax.ShapeDtypeStruct((rows, WIDTH_B), F32),
            jax.ShapeDtypeStruct((rows, WIDTH_C), F32),
            jax.ShapeDtypeStruct((batch, 1, WIDTH_B), F32),
            jax.ShapeDtypeStruct((batch, CONV_B - 1, WIDTH_B), F32),
            jax.ShapeDtypeStruct((batch, CONV_C - 1, WIDTH_C), F32),
        ],
        scratch_shapes=[
            pltpu.VMEM((tt + SUBLANES, WIDTH_B), F32),
            pltpu.VMEM((tt + SUBLANES, WIDTH_C), F32),
            pltpu.VMEM((tt, WIDTH_B), F32),
            pltpu.VMEM((tt, WIDTH_B), F32),
            pltpu.VMEM((1, WIDTH_B), F32),
        ],
        compiler_params=_params(("arbitrary", "arbitrary")),
        name="mix_p",
    )(zmix, *weights)


def _mix_s_kernel(z_ref, h0_ref, lch_ref, sch_ref, cw_ref, cb_ref, wa_ref, ba_ref, wx_ref, bx_ref,
                  lam_ref, sw_ref, ob_ref, oc_ref, hs_ref, pc_ref, *, t_new):
    tt = z_ref.shape[0]
    o3 = 2 * WIDTH_B
    tpos = lax.broadcasted_iota(jnp.int32, (tt, WIDTH_B), 0) % t_new

    def back(x, hist, s):
        return jnp.where(tpos >= s, pltpu.roll(x, s, 0), pltpu.roll(hist, tt - (t_new - s), 0))

    u = z_ref[:, 0:WIDTH_B]
    lch = lch_ref[...]
    xc = cw_ref[0:1, :] * back(u, lch, 3)
    xc = xc + cw_ref[1:2, :] * back(u, lch, 2)
    xc = xc + cw_ref[2:3, :] * back(u, lch, 1)
    xc = xc + cw_ref[3:4, :] * u
    xc = xc + cb_ref[...]
    a, b = _lru_coeffs(xc, wa_ref, ba_ref, wx_ref, bx_ref, lam_ref)
    b = b + a * h0_ref[...]
    for s in (1, 2, 4):
        keep = tpos >= s
        a_sh = jnp.where(keep, pltpu.roll(a, s, 0), 1.0)
        b_sh = jnp.where(keep, pltpu.roll(b, s, 0), 0.0)
        b = a * b_sh + b
        a = a * a_sh
    hs_ref[...] = b
    ob_ref[...] = b * jax.nn.gelu(z_ref[:, WIDTH_B:2 * WIDTH_B])

    pc = z_ref[:, o3 + WIDTH_C:o3 + 2 * WIDTH_C] * z_ref[:, o3 + 2 * WIDTH_C:o3 + 3 * WIDTH_C]
    sch = sch_ref[...]
    uc = sw_ref[0:1, :] * back(pc, sch, 2)
    uc = uc + sw_ref[1:2, :] * back(pc, sch, 1)
    uc = uc + sw_ref[2:3, :] * pc
    pc_ref[...] = pc
    oc_ref[...] = z_ref[:, o3:o3 + WIDTH_C] * uc


def _mix_s(zmix, h0e, lch, sch, weights, layer, row0, n_seq, t_new):
    ns = SEQ_PER_MIX_STEP
    tt = ns * t_new
    blk0 = row0 // tt
    rows = n_seq * t_new
    row_spec = lambda w_: pl.BlockSpec((tt, w_), lambda i: (i, 0))
    return pl.pallas_call(
        functools.partial(_mix_s_kernel, t_new=t_new),
        grid=(n_seq // ns,),
        in_specs=[
            pl.BlockSpec((tt, W_MIX), lambda i: (blk0 + i, 0)),
            row_spec(WIDTH_B), row_spec(WIDTH_B), row_spec(WIDTH_C),
        ] + _mix_weight_specs(layer, 1),
        out_specs=[row_spec(WIDTH_B), row_spec(WIDTH_C), row_spec(WIDTH_B), row_spec(WIDTH_C)],
        out_shape=[
            jax.ShapeDtypeStruct((rows, WIDTH_B), F32),
            jax.ShapeDtypeStruct((rows, WIDTH_C), F32),
            jax.ShapeDtypeStruct((rows, WIDTH_B), F32),
            jax.ShapeDtypeStruct((rows, WIDTH_C), F32),
        ],
        compiler_params=_params(("parallel",)),
        name="mix_s",
    )(zmix, h0e, lch, sch, *weights)


def _merge_kernel(x_ref, oa_ref, ob_ref, oc_ref, g_ref, wb_ref, wo_ref, o_ref):
    merged = None
    lo = 0
    for b, (o_b, width) in enumerate(((oa_ref, WIDTH_A), (ob_ref, WIDTH_B), (oc_ref, WIDTH_C))):
        proj = _dot(o_b[...].astype(BF16), wb_ref[lo:lo + width, :])
        term = jax.nn.sigmoid(g_ref[:, b * D_MODEL:(b + 1) * D_MODEL]) * proj
        merged = term if merged is None else merged + term
        lo += width
    o_ref[...] = x_ref[...] + _dot(merged.astype(BF16), wo_ref[...])


def _merge(x, oa, ob, oc, g, wb, wo, layer):
    rows = x.shape[0]
    row_spec = lambda w_: pl.BlockSpec((ROW_TILE, w_), lambda i: (i, 0))
    return pl.pallas_call(
        _merge_kernel,
        grid=(rows // ROW_TILE,),
        in_specs=[
            row_spec(D_MODEL), row_spec(WIDTH_A), row_spec(WIDTH_B), row_spec(WIDTH_C), row_spec(W_G),
            _resident((None, WIDTH_A + WIDTH_B + WIDTH_C, D_MODEL), lambda i: (layer, 0, 0)),
            _resident((None, D_MODEL, D_MODEL), lambda i: (layer, 0, 0)),
        ],
        out_specs=row_spec(D_MODEL),
        out_shape=jax.ShapeDtypeStruct((rows, D_MODEL), F32),
        compiler_params=_params(("parallel",)),
        name="merge",
    )(x, oa, ob, oc, g, wb, wo)


def kernel(x_prompt, x_sample, cache_k, cache_v, cache_logf, state_lru_h, state_lru_conv, state_sc_conv,
           page_table, ffn1_norm, ffn1_w_in, ffn1_w_out, mix_norm, w_in, b_forget, q_norm, k_norm,
           lru_conv_w, lru_conv_b, lru_wa, lru_ba, lru_wx, lru_bx, lru_lambda, sc_conv_w, w_branch, w_out,
           ffn2_norm, ffn2_w_in, ffn2_w_out):
    batch, seq, _ = x_prompt.shape
    n_seq, t_new, _ = x_sample.shape
    depth = w_in.shape[0]
    n_phys = cache_k.shape[0]
    rows_p = batch * seq
    rows_s = n_seq * t_new

    vec = lambda a: a[:, None, :]
    w1a, w2a = ffn1_w_in.astype(BF16), ffn1_w_out.astype(BF16)
    w1b, w2b = ffn2_w_in.astype(BF16), ffn2_w_out.astype(BF16)
    w_in_p = jnp.concatenate(
        [w_in[:, :, :OFF_F], w_in[:, :, OFF_MIX:], w_in[:, :, OFF_F:OFF_MIX],
         jnp.zeros((depth, D_MODEL, LANES - N_HEADS), w_in.dtype)], axis=-1).astype(BF16)
    bf_p = jnp.pad(b_forget, ((0, 0), (0, LANES - N_HEADS)))[:, None, :]
    qn_p = vec(jnp.tile(q_norm, (1, N_HEADS)))
    kn_p = vec(jnp.tile(k_norm, (1, N_HEADS)))
    eye = jnp.eye(N_BLOCKS_B, dtype=lru_wa.dtype)
    dense = lambda w: jnp.einsum('lgij,gh->lgihj', w, eye).reshape(depth, WIDTH_B, WIDTH_B).astype(BF16)
    mix_weights = (lru_conv_w, vec(lru_conv_b), dense(lru_wa), vec(lru_ba), dense(lru_wx), vec(lru_bx),
                   vec(lru_lambda), sc_conv_w)
    wb_b, wo_b = w_branch.astype(BF16), w_out.astype(BF16)
    ck4 = cache_k.reshape(n_phys, depth, PAGE, WIDTH_A)
    cv4 = cache_v.reshape(n_phys, depth, PAGE, WIDTH_A)
    clft = jnp.swapaxes(cache_logf, 2, 3)

    def seq_rows(state, at_end):
        r = state.shape[1]
        pad = (t_new - r, 0) if at_end else (0, t_new - r)
        return jnp.pad(state, ((0, 0), pad, (0, 0))).reshape(rows_s, state.shape[-1])

    x = jnp.concatenate([x_prompt.reshape(rows_p, D_MODEL), x_sample.reshape(rows_s, D_MODEL)], axis=0)
    outs_p = [[] for _ in range(6)]
    outs_s = [[] for _ in range(6)]
    for l in range(depth):
        x = _ffn(x, vec(ffn1_norm), w1a, w2a, l)
        q, k, v, qb, kb, vb, lf, lft, zmix, gates = _inproj(x, vec(mix_norm), w_in_p, bf_p, qn_p, kn_p, l)
        ct = _cumsum(lft, batch, seq)
        oa_p = _attn_p(qb, kb, vb, ct.T, ct, batch, seq)
        oa_s = _attn_s(page_table, q, k, v, lf, ck4, cv4, clft, l, rows_p, t_new)
        ob_p, oc_p, h_p, lc_p, sc_p = _mix_p(zmix, mix_weights, l, batch, seq)
        ob_s, oc_s, hs_s, pc_s = _mix_s(zmix, seq_rows(state_lru_h[:, l][:, None, :], False),
                                        seq_rows(state_lru_conv[:, l], True),
                                        seq_rows(state_sc_conv[:, l], True), mix_weights, l, rows_p, n_seq, t_new)
        per_seq = lambda a: a.reshape(n_seq, t_new, a.shape[-1])
        h_s = per_seq(hs_s)[:, -1]
        lc_s = per_seq(zmix[rows_p:, :WIDTH_B])[:, t_new - (CONV_B - 1):]
        sc_s = per_seq(pc_s)[:, t_new - (CONV_C - 1):]
        cat = lambda a, b: jnp.concatenate([a, b], axis=0)
        x = _merge(x, cat(oa_p, oa_s), cat(ob_p, ob_s), cat(oc_p, oc_s), gates, wb_b, wo_b, l)
        x = _ffn(x, vec(ffn2_norm), w1b, w2b, l)
        lf8 = lf[:, :N_HEADS]
        for dst, val in zip(outs_p, (k[:rows_p].reshape(batch, seq, N_HEADS, HEAD_DIM),
                                     v[:rows_p].reshape(batch, seq, N_HEADS, HEAD_DIM),
                                     lf8[:rows_p].reshape(batch, seq, N_HEADS),
                                     h_p[:, 0], lc_p, sc_p)):
            dst.append(val)
        for dst, val in zip(outs_s, (k[rows_p:].reshape(n_seq, t_new, N_HEADS, HEAD_DIM),
                                     v[rows_p:].reshape(n_seq, t_new, N_HEADS, HEAD_DIM),
                                     lf8[rows_p:].reshape(n_seq, t_new, N_HEADS),
                                     h_s, lc_s, sc_s)):
            dst.append(val)
    stack = lambda xs: jnp.stack(xs, axis=1)
    y_p = x[:rows_p].reshape(batch, seq, D_MODEL)
    y_s = x[rows_p:].reshape(n_seq, t_new, D_MODEL)
    return (y_p, y_s) + tuple(stack(o) for o in outs_p) + tuple(stack(o) for o in outs_s)
```

```python
import functools

import jax
import jax.numpy as jnp
from jax import lax
from jax.experimental import pallas as pl
from jax.experimental.pallas import tpu as pltpu

F32 = jnp.float32
BF16 = jnp.bfloat16

D_MODEL = 1024
N_HEADS = 8
HEAD_DIM = 64
WIDTH_A = N_HEADS * HEAD_DIM
WIDTH_B = 512
N_BLOCKS_B = 8
CONV_B = 4
LRU_C = 8.0
WIDTH_C = 512
CONV_C = 3
N_BRANCH = 3
FFN_HIDDEN = 2816
RMS_EPS = 1e-6
PAGE = 128

LANES = 128
SUBLANES = 8
VMEM_LIMIT = 56 * 1024 * 1024

OFF_F = 3 * WIDTH_A
OFF_MIX = OFF_F + N_HEADS
W_MIX = 2 * WIDTH_B + 3 * WIDTH_C
OFF_G = OFF_MIX + W_MIX
W_G = N_BRANCH * D_MODEL
P_MIX = 3 * WIDTH_A
P_G = P_MIX + W_MIX
P_F = P_G + W_G
P_END = P_F + LANES

NEG = -1e30
ROW_TILE = 512
FFN_CHUNK = 1408
ATT_TILE = 512
MIX_TILE = 512
SEQ_PER_MIX_STEP = 64


def _params(sem):
    return pltpu.CompilerParams(dimension_semantics=sem, vmem_limit_bytes=VMEM_LIMIT)


def _resident(shape, index_map):
    return pl.BlockSpec(shape, index_map, pipeline_mode=pl.Buffered(1))


def _rms(x, g):
    return (x * lax.rsqrt(jnp.mean(x * x, axis=-1, keepdims=True) + RMS_EPS)) * g


def _softplus(x):
    return jnp.maximum(x, 0.0) + jnp.log1p(jnp.exp(-jnp.abs(x)))


def _dot(a, b):
    return jnp.dot(a, b, preferred_element_type=F32)


def _dot_nt(a, b):
    return lax.dot_general(a, b, (((1,), (1,)), ((), ())), preferred_element_type=F32)


def _ffn_kernel(x_ref, g_ref, w1_ref, w2_ref, o_ref):
    x = x_ref[...]
    xn = _rms(x, g_ref[...]).astype(BF16)
    acc = jnp.zeros(x.shape, F32)
    for c in range(FFN_HIDDEN // FFN_CHUNK):
        lo = c * FFN_CHUNK
        gate = _dot(xn, w1_ref[:, lo:lo + FFN_CHUNK])
        up = _dot(xn, w1_ref[:, FFN_HIDDEN + lo:FFN_HIDDEN + lo + FFN_CHUNK])
        act = (gate * jax.nn.sigmoid(gate)) * up
        acc = acc + _dot(act.astype(BF16), w2_ref[lo:lo + FFN_CHUNK, :])
    o_ref[...] = x + 0.5 * acc


def _ffn(x, norm, w1, w2, layer):
    rows = x.shape[0]
    return pl.pallas_call(
        _ffn_kernel,
        grid=(rows // ROW_TILE,),
        in_specs=[
            pl.BlockSpec((ROW_TILE, D_MODEL), lambda i: (i, 0)),
            pl.BlockSpec((None, 1, D_MODEL), lambda i: (layer, 0, 0)),
            _resident((None, D_MODEL, 2 * FFN_HIDDEN), lambda i: (layer, 0, 0)),
            _resident((None, FFN_HIDDEN, D_MODEL), lambda i: (layer, 0, 0)),
        ],
        out_specs=pl.BlockSpec((ROW_TILE, D_MODEL), lambda i: (i, 0)),
        out_shape=jax.ShapeDtypeStruct((rows, D_MODEL), F32),
        compiler_params=_params(("parallel",)),
        name="ffn",
    )(x, norm, w1, w2)


def _head_rms(x, gamma):
    m = x.shape[0]
    lane = lax.broadcasted_iota(jnp.int32, (m, LANES), 1)
    first = lane < HEAD_DIM
    outs = []
    for c in range(WIDTH_A // LANES):
        xc = x[:, c * LANES:(c + 1) * LANES]
        sq = xc * xc
        s_lo = jnp.sum(jnp.where(first, sq, 0.0), axis=-1, keepdims=True)
        s_hi = jnp.sum(jnp.where(first, 0.0, sq), axis=-1, keepdims=True)
        ms = jnp.where(first, s_lo, s_hi) * (1.0 / HEAD_DIM)
        outs.append((xc * lax.rsqrt(ms + RMS_EPS)) * gamma[:, c * LANES:(c + 1) * LANES])
    return jnp.concatenate(outs, axis=-1)


def _inproj_kernel(x_ref, g_ref, w_ref, bf_ref, qn_ref, kn_ref,
                   q_ref, k_ref, v_ref, qb_ref, kb_ref, vb_ref, lf_ref, lft_ref, mix_ref, gate_ref):
    xn = _rms(x_ref[...], g_ref[...]).astype(BF16)
    qkv = _dot(xn, w_ref[:, 0:P_MIX])
    q = _head_rms(qkv[:, 0:WIDTH_A], qn_ref[...]) * (HEAD_DIM ** -0.5)
    k = _head_rms(qkv[:, WIDTH_A:2 * WIDTH_A], kn_ref[...])
    v = qkv[:, 2 * WIDTH_A:3 * WIDTH_A]
    q_ref[...] = q
    k_ref[...] = k
    v_ref[...] = v
    qb_ref[...] = q.astype(BF16)
    kb_ref[...] = k.astype(BF16)
    vb_ref[...] = v.astype(BF16)
    mix_ref[...] = _dot(xn, w_ref[:, P_MIX:P_G])
    gate_ref[...] = _dot(xn, w_ref[:, P_G:P_F])
    f = _dot(xn, w_ref[:, P_F:P_END]) + bf_ref[...]
    lf = jnp.minimum(f, 0.0) - jnp.log1p(jnp.exp(-jnp.abs(f)))
    lf_ref[...] = lf
    lft_ref[...] = lf.T[0:N_HEADS, :]


def _inproj(x, norm, w, bf, qn, kn, layer):
    rows = x.shape[0]
    row_spec = lambda w_: pl.BlockSpec((ROW_TILE, w_), lambda i: (i, 0))
    vec_spec = lambda w_: pl.BlockSpec((None, 1, w_), lambda i: (layer, 0, 0))
    return pl.pallas_call(
        _inproj_kernel,
        grid=(rows // ROW_TILE,),
        in_specs=[
            row_spec(D_MODEL),
            vec_spec(D_MODEL),
            _resident((None, D_MODEL, P_END), lambda i: (layer, 0, 0)),
            vec_spec(LANES),
            vec_spec(WIDTH_A),
            vec_spec(WIDTH_A),
        ],
        out_specs=[
            row_spec(WIDTH_A), row_spec(WIDTH_A), row_spec(WIDTH_A),
            row_spec(WIDTH_A), row_spec(WIDTH_A), row_spec(WIDTH_A),
            row_spec(LANES),
            pl.BlockSpec((N_HEADS, ROW_TILE), lambda i: (0, i)),
            row_spec(W_MIX),
            row_spec(W_G),
        ],
        out_shape=[
            jax.ShapeDtypeStruct((rows, WIDTH_A), F32),
            jax.ShapeDtypeStruct((rows, WIDTH_A), F32),
            jax.ShapeDtypeStruct((rows, WIDTH_A), F32),
            jax.ShapeDtypeStruct((rows, WIDTH_A), BF16),
            jax.ShapeDtypeStruct((rows, WIDTH_A), BF16),
            jax.ShapeDtypeStruct((rows, WIDTH_A), BF16),
            jax.ShapeDtypeStruct((rows, LANES), F32),
            jax.ShapeDtypeStruct((N_HEADS, rows), F32),
            jax.ShapeDtypeStruct((rows, W_MIX), F32),
            jax.ShapeDtypeStruct((rows, W_G), F32),
        ],
        compiler_params=_params(("parallel",)),
        name="inproj",
    )(x, norm, w, bf, qn, kn)


def _lane_prefix(x, n_valid):
    lane = lax.broadcasted_iota(jnp.int32, x.shape, 1)
    shift = 1
    while shift < n_valid:
        x = x + jnp.where(lane >= shift, pltpu.roll(x, shift, 1), 0.0)
        shift *= 2
    return x


def _cumsum_kernel(x_ref, o_ref):
    o_ref[...] = _lane_prefix(x_ref[...], x_ref.shape[1])


def _cumsum(lft, batch, seq):
    return pl.pallas_call(
        _cumsum_kernel,
        grid=(batch,),
        in_specs=[pl.BlockSpec((N_HEADS, seq), lambda b: (0, b))],
        out_specs=pl.BlockSpec((N_HEADS, seq), lambda b: (0, b)),
        out_shape=jax.ShapeDtypeStruct((N_HEADS, batch * seq), F32),
        compiler_params=_params(("parallel",)),
        name="cumsum",
    )(lft)


def _attn_p_kernel(q_ref, k_ref, v_ref, c_ref, ct_ref, o_ref, m_ref, acc_ref, cq_ref, vx_ref):
    t = ATT_TILE
    hp = pl.program_id(1)
    qi = pl.program_id(2)
    lane = lax.broadcasted_iota(jnp.int32, (t, LANES), 1)
    first = lane < HEAD_DIM

    @pl.when(qi == 0)
    def _():
        v = v_ref[...]
        own = lax.broadcasted_iota(jnp.int32, v.shape, 1) < HEAD_DIM
        one = jnp.ones_like(v)
        vx_ref[0] = jnp.where(own, v, one)
        vx_ref[1] = jnp.where(own, one, v)

    q2 = q_ref[...]
    zero = jnp.zeros_like(q2)
    q_heads = (jnp.where(first, q2, zero), jnp.where(first, zero, q2))
    crow = c_ref[...]
    hlane = lax.broadcasted_iota(jnp.int32, crow.shape, 1)
    for e in range(2):
        cq = jnp.sum(jnp.where(hlane == 2 * hp + e, crow, 0.0), axis=-1, keepdims=True)
        cq_ref[e] = jnp.broadcast_to(cq, (t, LANES))
    m_ref[...] = jnp.full(m_ref.shape, NEG, F32)
    acc_ref[...] = jnp.zeros(acc_ref.shape, F32)
    row = lax.broadcasted_iota(jnp.int32, (t, t), 0)
    col = lax.broadcasted_iota(jnp.int32, (t, t), 1)
    causal = col <= row
    wide = lambda x: jnp.concatenate([x] * (t // LANES), axis=1)

    def step(j, masked):
        start = pl.multiple_of(j * t, t)
        kc = k_ref[pl.ds(start, t), :]
        for e in range(2):
            ck = ct_ref[pl.ds(2 * hp + e, 1), pl.ds(start, t)]
            s = (_dot_nt(q_heads[e], kc) + wide(cq_ref[e])) - ck
            if masked:
                s = jnp.where(causal, s, NEG)
            m_prev = m_ref[e]
            m_new = jnp.maximum(m_prev, jnp.max(s, axis=-1, keepdims=True))
            alpha = jnp.exp(m_prev - m_new)
            p = jnp.exp(s - wide(m_new))
            acc_ref[e] = alpha * acc_ref[e] + _dot(p.astype(BF16), vx_ref[e, pl.ds(start, t), :])
            m_ref[e] = m_new

    def body(j, carry):
        step(j, False)
        return carry

    lax.fori_loop(0, qi, body, 0)
    step(qi, True)
    a0 = acc_ref[0]
    a1 = acc_ref[1]
    o_ref[...] = jnp.where(first, a0 / pltpu.roll(a0, HEAD_DIM, 1), a1 / pltpu.roll(a1, HEAD_DIM, 1))


def _attn_p(qb, kb, vb, c_rows, ct, batch, seq):
    t = ATT_TILE
    nq = seq // t
    return pl.pallas_call(
        _attn_p_kernel,
        grid=(batch, WIDTH_A // LANES, nq),
        in_specs=[
            pl.BlockSpec((t, LANES), lambda b, h, i: (b * nq + i, h)),
            pl.BlockSpec((seq, LANES), lambda b, h, i: (b, h)),
            pl.BlockSpec((seq, LANES), lambda b, h, i: (b, h)),
            pl.BlockSpec((t, N_HEADS), lambda b, h, i: (b * nq + i, 0)),
            pl.BlockSpec((N_HEADS, seq), lambda b, h, i: (0, b)),
        ],
        out_specs=pl.BlockSpec((t, LANES), lambda b, h, i: (b * nq + i, h)),
        out_shape=jax.ShapeDtypeStruct((qb.shape[0], WIDTH_A), F32),
        scratch_shapes=[
            pltpu.VMEM((2, t, LANES), F32),
            pltpu.VMEM((2, t, LANES), F32),
            pltpu.VMEM((2, t, LANES), F32),
            pltpu.VMEM((2, seq, LANES), BF16),
        ],
        compiler_params=_params(("parallel", "parallel", "arbitrary")),
        name="attn_p",
    )(qb, kb, vb, c_rows, ct)


def _transpose_rows8(x):
    pad = jnp.concatenate([x, jnp.zeros((LANES - SUBLANES, LANES), F32)], axis=0)
    return pad.T[0:SUBLANES, :]


def _attn_s_kernel(pt_ref, q_ref, kn_ref, vn_ref, lfn_ref, *rest, n_pages, t_new):
    k_refs = rest[0:n_pages]
    v_refs = rest[n_pages:2 * n_pages]
    lf_refs = rest[2 * n_pages:3 * n_pages]
    o_ref = rest[3 * n_pages + 1]
    rows = N_HEADS * t_new
    rhead = lax.broadcasted_iota(jnp.int32, (rows, WIDTH_A), 0) // t_new
    chead = lax.broadcasted_iota(jnp.int32, (rows, WIDTH_A), 1) // HEAD_DIM
    diag = rhead == chead
    q = q_ref[...]
    qbd = jnp.where(diag, jnp.concatenate([q] * N_HEADS, axis=0), 0.0).astype(BF16)

    def rep(ct):
        return jnp.broadcast_to(ct[:, None, :], (N_HEADS, t_new, LANES)).reshape(rows, LANES)

    carry = jnp.zeros((N_HEADS, 1), F32)
    cts = []
    for j in range(n_pages):
        ct = _lane_prefix(lf_refs[j][...], PAGE) + carry
        cts.append(ct)
        carry = ct[:, PAGE - 1:PAGE]
    lft_new = _transpose_rows8(lfn_ref[...])
    ct_new = _lane_prefix(lft_new, t_new) + carry
    c_new = _transpose_rows8(ct_new)
    hl = lax.broadcasted_iota(jnp.int32, (rows, LANES), 1)
    rh = lax.broadcasted_iota(jnp.int32, (rows, LANES), 0) // t_new
    cq = jnp.sum(jnp.where(hl == rh, jnp.concatenate([c_new] * N_HEADS, axis=0), 0.0),
                 axis=-1, keepdims=True)

    s_blocks = []
    def page(ref):
        return ref[...].reshape(WIDTH_A, PAGE).astype(BF16)

    for j in range(n_pages):
        s = _dot(qbd, page(k_refs[j]))
        s_blocks.append((s + cq) - rep(cts[j]))
    zeros_pad = jnp.zeros((PAGE - t_new, WIDTH_A), F32)
    k_new = jnp.concatenate([kn_ref[...], zeros_pad], axis=0).astype(BF16)
    v_new = jnp.concatenate([vn_ref[...], zeros_pad], axis=0).astype(BF16)
    s = (_dot_nt(qbd, k_new) + cq) - rep(ct_new)
    tq = lax.broadcasted_iota(jnp.int32, (rows, LANES), 0) % t_new
    s_blocks.append(jnp.where(hl <= tq, s, NEG))

    m = s_blocks[0]
    for s in s_blocks[1:]:
        m = jnp.maximum(m, s)
    m = jnp.max(m, axis=-1, keepdims=True)
    l = jnp.zeros((rows, 1), F32)
    acc = jnp.zeros((rows, WIDTH_A), F32)
    for j in range(n_pages + 1):
        p = jnp.exp(s_blocks[j] - m)
        l = l + jnp.sum(p, axis=-1, keepdims=True)
        pb = p.astype(BF16)
        acc = acc + (_dot(pb, v_new) if j == n_pages else _dot_nt(pb, page(v_refs[j])))
    o = jnp.where(diag, acc / l, 0.0)
    out = o[0:t_new, :]
    for h in range(1, N_HEADS):
        out = out + o[h * t_new:(h + 1) * t_new, :]
    o_ref[...] = out


def _attn_s(page_table, q, k, v, lf, cache_k, cache_v, cache_lft, oa, layer, row0, t_new):
    n_seq, n_pages = page_table.shape
    blk0 = row0 // t_new
    new_spec = lambda w_: pl.BlockSpec((t_new, w_), lambda n, pt: (blk0 + n, 0))

    def page_spec(j, *dims):
        zeros = (0,) * len(dims)
        return pl.BlockSpec((None, None) + dims, lambda n, pt: (pt[n, j], layer) + zeros)

    in_specs = [new_spec(WIDTH_A), new_spec(WIDTH_A), new_spec(WIDTH_A), new_spec(LANES)]
    in_specs += [page_spec(j, N_HEADS, HEAD_DIM, PAGE) for j in range(n_pages)]
    in_specs += [page_spec(j, N_HEADS, HEAD_DIM, PAGE) for j in range(n_pages)]
    in_specs += [page_spec(j, N_HEADS, PAGE) for j in range(n_pages)]
    in_specs += [pl.BlockSpec(memory_space=pl.ANY)]
    grid_spec = pltpu.PrefetchScalarGridSpec(
        num_scalar_prefetch=1,
        grid=(n_seq,),
        in_specs=in_specs,
        out_specs=new_spec(WIDTH_A),
    )
    return pl.pallas_call(
        functools.partial(_attn_s_kernel, n_pages=n_pages, t_new=t_new),
        grid_spec=grid_spec,
        out_shape=jax.ShapeDtypeStruct(oa.shape, oa.dtype),
        input_output_aliases={len(in_specs): 0},
        compiler_params=_params(("parallel",)),
        name="attn_s",
    )(page_table, q, k, v, lf, *([cache_k] * n_pages), *([cache_v] * n_pages), *([cache_lft] * n_pages), oa)


def _lru_coeffs(xc, wa_ref, ba_ref, wx_ref, bx_ref, lam_ref):
    xb = xc.astype(BF16)
    r = jax.nn.sigmoid(_dot(xb, wa_ref[...]) + ba_ref[...])
    ig = jax.nn.sigmoid(_dot(xb, wx_ref[...]) + bx_ref[...])
    log_a = (-LRU_C * r) * _softplus(-lam_ref[...])
    a = jnp.exp(log_a)
    b = jnp.sqrt(1.0 - a * a) * (ig * xc)
    return a, b


def _mix_p_kernel(z_ref, cw_ref, cb_ref, wa_ref, ba_ref, wx_ref, bx_ref, lam_ref, sw_ref,
                  ob_ref, oc_ref, h_ref, lc_ref, scc_ref,
                  ubuf, pbuf, a_s, b_s, hcar):
    tt = MIX_TILE
    hist = SUBLANES
    i = pl.program_id(1)

    @pl.when(i == 0)
    def _():
        ubuf[0:hist, :] = jnp.zeros((hist, WIDTH_B), F32)
        pbuf[0:hist, :] = jnp.zeros((hist, WIDTH_C), F32)
        hcar[...] = jnp.zeros(hcar.shape, F32)

    u = z_ref[:, 0:WIDTH_B]
    ubuf[hist:hist + tt, :] = u
    xc = cw_ref[0:1, :] * ubuf[hist - 3:hist - 3 + tt, :]
    xc = xc + cw_ref[1:2, :] * ubuf[hist - 2:hist - 2 + tt, :]
    xc = xc + cw_ref[2:3, :] * ubuf[hist - 1:hist - 1 + tt, :]
    xc = xc + cw_ref[3:4, :] * u
    xc = xc + cb_ref[...]
    ubuf[0:hist, :] = ubuf[tt:tt + hist, :]
    lc_ref[...] = z_ref[tt - (CONV_B - 1):tt, 0:WIDTH_B]

    a, b = _lru_coeffs(xc, wa_ref, ba_ref, wx_ref, bx_ref, lam_ref)
    a_s[...] = a
    b_s[...] = b
    srow = lax.broadcasted_iota(jnp.int32, (SUBLANES, WIDTH_B), 0)

    def chunk(c, h):
        r0 = pl.multiple_of(c * SUBLANES, SUBLANES)
        aa = a_s[pl.ds(r0, SUBLANES), :]
        bb = b_s[pl.ds(r0, SUBLANES), :]
        for s in (1, 2, 4):
            keep = srow >= s
            a_sh = jnp.where(keep, pltpu.roll(aa, s, 0), 1.0)
            b_sh = jnp.where(keep, pltpu.roll(bb, s, 0), 0.0)
            bb = aa * b_sh + bb
            aa = aa * a_sh
        hc = aa * h + bb
        b_s[pl.ds(r0, SUBLANES), :] = hc
        return hc[SUBLANES - 1:SUBLANES, :]

    h_last = lax.fori_loop(0, tt // SUBLANES, chunk, hcar[...], unroll=4)
    hcar[...] = h_last
    h_ref[...] = h_last
    ob_ref[...] = b_s[...] * jax.nn.gelu(z_ref[:, WIDTH_B:2 * WIDTH_B])

    o3 = 2 * WIDTH_B
    pc = z_ref[:, o3 + WIDTH_C:o3 + 2 * WIDTH_C] * z_ref[:, o3 + 2 * WIDTH_C:o3 + 3 * WIDTH_C]
    pbuf[hist:hist + tt, :] = pc
    uc = sw_ref[0:1, :] * pbuf[hist - 2:hist - 2 + tt, :]
    uc = uc + sw_ref[1:2, :] * pbuf[hist - 1:hist - 1 + tt, :]
    uc = uc + sw_ref[2:3, :] * pc
    pbuf[0:hist, :] = pbuf[tt:tt + hist, :]
    scc_ref[...] = pbuf[hist + tt - (CONV_C - 1):hist + tt, :]
    oc_ref[...] = z_ref[:, o3:o3 + WIDTH_C] * uc


def _mix_weight_specs(layer, n_grid):
    zeros = (0,) * 2

    def im(*_):
        return (layer,) + zeros

    return [
        pl.BlockSpec((None, CONV_B, WIDTH_B), im),
        pl.BlockSpec((None, 1, WIDTH_B), im),
        pl.BlockSpec((None, WIDTH_B, WIDTH_B), im),
        pl.BlockSpec((None, 1, WIDTH_B), im),
        pl.BlockSpec((None, WIDTH_B, WIDTH_B), im),
        pl.BlockSpec((None, 1, WIDTH_B), im),
        pl.BlockSpec((None, 1, WIDTH_B), im),
        pl.BlockSpec((None, CONV_C, WIDTH_C), im),
    ]


def _mix_p(zmix, weights, layer, batch, seq):
    tt = MIX_TILE
    nt = seq // tt
    rows = zmix.shape[0]
    return pl.pallas_call(
        _mix_p_kernel,
        grid=(batch, nt),
        in_specs=[pl.BlockSpec((tt, W_MIX), lambda b, i: (b * nt + i, 0))] + _mix_weight_specs(layer, 2),
        out_specs=[
            pl.BlockSpec((tt, WIDTH_B), lambda b, i: (b * nt + i, 0)),
            pl.BlockSpec((tt, WIDTH_C), lambda b, i: (b * nt + i, 0)),
            pl.BlockSpec((None, 1, WIDTH_B), lambda b, i: (b, 0, 0)),
            pl.BlockSpec((None, CONV_B - 1, WIDTH_B), lambda b, i: (b, 0, 0)),
            pl.BlockSpec((None, CONV_C - 1, WIDTH_C), lambda b, i: (b, 0, 0)),
        ],
        out_shape=[
            jax.ShapeDtypeStruct((rows, WIDTH_B), F32),
            jax.ShapeDtypeStruct((rows, WIDTH_C), F32),
            jax.ShapeDtypeStruct((batch, 1, WIDTH_B), F32),
            jax.ShapeDtypeStruct((batch, CONV_B - 1, WIDTH_B), F32),
            jax.ShapeDtypeStruct((batch, CONV_C - 1, WIDTH_C), F32),
        ],
        scratch_shapes=[
            pltpu.VMEM((tt + SUBLANES, WIDTH_B), F32),
            pltpu.VMEM((tt + SUBLANES, WIDTH_C), F32),
            pltpu.VMEM((tt, WIDTH_B), F32),
            pltpu.VMEM((tt, WIDTH_B), F32),
            pltpu.VMEM((1, WIDTH_B), F32),
        ],
        compiler_params=_params(("arbitrary", "arbitrary")),
        name="mix_p",
    )(zmix, *weights)


def _mix_s_kernel(z_ref, h0_ref, lch_ref, sch_ref, cw_ref, cb_ref, wa_ref, ba_ref, wx_ref, bx_ref,
                  lam_ref, sw_ref, ob_alias, oc_alias, ob_ref, oc_ref, hs_ref, pc_ref, *, t_new):
    tt = z_ref.shape[0]
    o3 = 2 * WIDTH_B
    tpos = lax.broadcasted_iota(jnp.int32, (tt, WIDTH_B), 0) % t_new

    def back(x, hist, s):
        return jnp.where(tpos >= s, pltpu.roll(x, s, 0), pltpu.roll(hist, tt - (t_new - s), 0))

    u = z_ref[:, 0:WIDTH_B]
    lch = lch_ref[...]
    xc = cw_ref[0:1, :] * back(u, lch, 3)
    xc = xc + cw_ref[1:2, :] * back(u, lch, 2)
    xc = xc + cw_ref[2:3, :] * back(u, lch, 1)
    xc = xc + cw_ref[3:4, :] * u
    xc = xc + cb_ref[...]
    a, b = _lru_coeffs(xc, wa_ref, ba_ref, wx_ref, bx_ref, lam_ref)
    b = b + a * h0_ref[...]
    for s in (1, 2, 4):
        keep = tpos >= s
        a_sh = jnp.where(keep, pltpu.roll(a, s, 0), 1.0)
        b_sh = jnp.where(keep, pltpu.roll(b, s, 0), 0.0)
        b = a * b_sh + b
        a = a * a_sh
    hs_ref[...] = b
    ob_ref[...] = b * jax.nn.gelu(z_ref[:, WIDTH_B:2 * WIDTH_B])

    pc = z_ref[:, o3 + WIDTH_C:o3 + 2 * WIDTH_C] * z_ref[:, o3 + 2 * WIDTH_C:o3 + 3 * WIDTH_C]
    sch = sch_ref[...]
    uc = sw_ref[0:1, :] * back(pc, sch, 2)
    uc = uc + sw_ref[1:2, :] * back(pc, sch, 1)
    uc = uc + sw_ref[2:3, :] * pc
    pc_ref[...] = pc
    oc_ref[...] = z_ref[:, o3:o3 + WIDTH_C] * uc


def _mix_s(zmix, h0e, lch, sch, weights, ob, oc, layer, row0, n_seq, t_new):
    ns = SEQ_PER_MIX_STEP
    tt = ns * t_new
    blk0 = row0 // tt
    rows = n_seq * t_new
    row_spec = lambda w_: pl.BlockSpec((tt, w_), lambda i: (i, 0))
    all_spec = lambda w_: pl.BlockSpec((tt, w_), lambda i: (blk0 + i, 0))
    in_specs = [all_spec(W_MIX), row_spec(WIDTH_B), row_spec(WIDTH_B), row_spec(WIDTH_C)]
    in_specs += _mix_weight_specs(layer, 1)
    in_specs += [pl.BlockSpec(memory_space=pl.ANY)] * 2
    return pl.pallas_call(
        functools.partial(_mix_s_kernel, t_new=t_new),
        grid=(n_seq // ns,),
        in_specs=in_specs,
        out_specs=[all_spec(WIDTH_B), all_spec(WIDTH_C), row_spec(WIDTH_B), row_spec(WIDTH_C)],
        out_shape=[
            jax.ShapeDtypeStruct(ob.shape, ob.dtype),
            jax.ShapeDtypeStruct(oc.shape, oc.dtype),
            jax.ShapeDtypeStruct((rows, WIDTH_B), F32),
            jax.ShapeDtypeStruct((rows, WIDTH_C), F32),
        ],
        input_output_aliases={len(in_specs) - 2: 0, len(in_specs) - 1: 1},
        compiler_params=_params(("parallel",)),
        name="mix_s",
    )(zmix, h0e, lch, sch, *weights, ob, oc)


def _merge_kernel(x_ref, oa_ref, ob_ref, oc_ref, g_ref, wb_ref, wo_ref, o_ref):
    merged = None
    lo = 0
    for b, (o_b, width) in enumerate(((oa_ref, WIDTH_A), (ob_ref, WIDTH_B), (oc_ref, WIDTH_C))):
        proj = _dot(o_b[...].astype(BF16), wb_ref[lo:lo + width, :])
        term = jax.nn.sigmoid(g_ref[:, b * D_MODEL:(b + 1) * D_MODEL]) * proj
        merged = term if merged is None else merged + term
        lo += width
    o_ref[...] = x_ref[...] + _dot(merged.astype(BF16), wo_ref[...])


def _merge(x, oa, ob, oc, g, wb, wo, layer):
    rows = x.shape[0]
    row_spec = lambda w_: pl.BlockSpec((ROW_TILE, w_), lambda i: (i, 0))
    return pl.pallas_call(
        _merge_kernel,
        grid=(rows // ROW_TILE,),
        in_specs=[
            row_spec(D_MODEL), row_spec(WIDTH_A), row_spec(WIDTH_B), row_spec(WIDTH_C), row_spec(W_G),
            _resident((None, WIDTH_A + WIDTH_B + WIDTH_C, D_MODEL), lambda i: (layer, 0, 0)),
            _resident((None, D_MODEL, D_MODEL), lambda i: (layer, 0, 0)),
        ],
        out_specs=row_spec(D_MODEL),
        out_shape=jax.ShapeDtypeStruct((rows, D_MODEL), F32),
        compiler_params=_params(("parallel",)),
        name="merge",
    )(x, oa, ob, oc, g, wb, wo)


def kernel(x_prompt, x_sample, cache_k, cache_v, cache_logf, state_lru_h, state_lru_conv, state_sc_conv,
           page_table, ffn1_norm, ffn1_w_in, ffn1_w_out, mix_norm, w_in, b_forget, q_norm, k_norm,
           lru_conv_w, lru_conv_b, lru_wa, lru_ba, lru_wx, lru_bx, lru_lambda, sc_conv_w, w_branch, w_out,
           ffn2_norm, ffn2_w_in, ffn2_w_out):
    batch, seq, _ = x_prompt.shape
    n_seq, t_new, _ = x_sample.shape
    depth = w_in.shape[0]
    n_phys = cache_k.shape[0]
    rows_p = batch * seq
    rows_s = n_seq * t_new

    vec = lambda a: a[:, None, :]
    w1a, w2a = ffn1_w_in.astype(BF16), ffn1_w_out.astype(BF16)
    w1b, w2b = ffn2_w_in.astype(BF16), ffn2_w_out.astype(BF16)
    w_in_p = jnp.concatenate(
        [w_in[:, :, :OFF_F], w_in[:, :, OFF_MIX:], w_in[:, :, OFF_F:OFF_MIX],
         jnp.zeros((depth, D_MODEL, LANES - N_HEADS), w_in.dtype)], axis=-1).astype(BF16)
    bf_p = jnp.pad(b_forget, ((0, 0), (0, LANES - N_HEADS)))[:, None, :]
    qn_p = vec(jnp.tile(q_norm, (1, N_HEADS)))
    kn_p = vec(jnp.tile(k_norm, (1, N_HEADS)))
    eye = jnp.eye(N_BLOCKS_B, dtype=lru_wa.dtype)
    dense = lambda w: jnp.einsum('lgij,gh->lgihj', w, eye).reshape(depth, WIDTH_B, WIDTH_B).astype(BF16)
    mix_weights = (lru_conv_w, vec(lru_conv_b), dense(lru_wa), vec(lru_ba), dense(lru_wx), vec(lru_bx),
                   vec(lru_lambda), sc_conv_w)
    wb_b, wo_b = w_branch.astype(BF16), w_out.astype(BF16)
    ck4 = jnp.transpose(cache_k, (0, 1, 3, 4, 2))
    cv4 = jnp.transpose(cache_v, (0, 1, 3, 4, 2))
    clft = jnp.swapaxes(cache_logf, 2, 3)

    def seq_rows(state, at_end):
        r = state.shape[1]
        pad = (t_new - r, 0) if at_end else (0, t_new - r)
        return jnp.pad(state, ((0, 0), pad, (0, 0))).reshape(rows_s, state.shape[-1])

    x = jnp.concatenate([x_prompt.reshape(rows_p, D_MODEL), x_sample.reshape(rows_s, D_MODEL)], axis=0)
    outs_p = [[] for _ in range(6)]
    outs_s = [[] for _ in range(6)]
    for l in range(depth):
        x = _ffn(x, vec(ffn1_norm), w1a, w2a, l)
        q, k, v, qb, kb, vb, lf, lft, zmix, gates = _inproj(x, vec(mix_norm), w_in_p, bf_p, qn_p, kn_p, l)
        ct = _cumsum(lft, batch, seq)
        oa = _attn_p(qb, kb, vb, ct.T, ct, batch, seq)
        oa = _attn_s(page_table, q, k, v, lf, ck4, cv4, clft, oa, l, rows_p, t_new)
        ob, oc, h_p, lc_p, sc_p = _mix_p(zmix, mix_weights, l, batch, seq)
        ob, oc, hs_s, pc_s = _mix_s(zmix, seq_rows(state_lru_h[:, l][:, None, :], False),
                                    seq_rows(state_lru_conv[:, l], True),
                                    seq_rows(state_sc_conv[:, l], True), mix_weights, ob, oc,
                                    l, rows_p, n_seq, t_new)
        per_seq = lambda a: a.reshape(n_seq, t_new, a.shape[-1])
        h_s = per_seq(hs_s)[:, -1]
        lc_s = per_seq(zmix[rows_p:, :WIDTH_B])[:, t_new - (CONV_B - 1):]
        sc_s = per_seq(pc_s)[:, t_new - (CONV_C - 1):]
        x = _merge(x, oa, ob, oc, gates, wb_b, wo_b, l)
        x = _ffn(x, vec(ffn2_norm), w1b, w2b, l)
        lf8 = lf[:, :N_HEADS]
        for dst, val in zip(outs_p, (k[:rows_p].reshape(batch, seq, N_HEADS, HEAD_DIM),
                                     v[:rows_p].reshape(batch, seq, N_HEADS, HEAD_DIM),
                                     lf8[:rows_p].reshape(batch, seq, N_HEADS),
                                     h_p[:, 0], lc_p, sc_p)):
            dst.append(val)
        for dst, val in zip(outs_s, (k[rows_p:].reshape(n_seq, t_new, N_HEADS, HEAD_DIM),
                                     v[rows_p:].reshape(n_seq, t_new, N_HEADS, HEAD_DIM),
                                     lf8[rows_p:].reshape(n_seq, t_new, N_HEADS),
                                     h_s, lc_s, sc_s)):
            dst.append(val)
    stack = lambda xs: jnp.stack(xs, axis=1)
    y_p = x[:rows_p].reshape(batch, seq, D_MODEL)
    y_s = x[rows_p:].reshape(n_seq, t_new, D_MODEL)
    return (y_p, y_s) + tuple(stack(o) for o in outs_p) + tuple(stack(o) for o in outs_s)
```

```python
import functools
import math

import jax
import jax.numpy as jnp
from jax import lax
from jax.experimental import pallas as pl
from jax.experimental.pallas import tpu as pltpu

F32 = jnp.float32
BF16 = jnp.bfloat16

D_MODEL = 1024
N_HEADS = 8
HEAD_DIM = 64
WIDTH_A = N_HEADS * HEAD_DIM
WIDTH_B = 512
N_BLOCKS_B = 8
CONV_B = 4
LRU_C = 8.0
WIDTH_C = 512
CONV_C = 3
N_BRANCH = 3
FFN_HIDDEN = 2816
RMS_EPS = 1e-6
PAGE = 128

LANES = 128
SUBLANES = 8
VMEM_LIMIT = 56 * 1024 * 1024
LOG2E = math.log2(math.e)

OFF_F = 3 * WIDTH_A
OFF_MIX = OFF_F + N_HEADS
W_MIX = 2 * WIDTH_B + 3 * WIDTH_C
OFF_G = OFF_MIX + W_MIX
W_G = N_BRANCH * D_MODEL
P_MIX = 3 * WIDTH_A
P_G = P_MIX + W_MIX
P_F = P_G + W_G
P_END = P_F + LANES

NEG = -1e30
ROW_TILE = 512
FFN_CHUNK = 1408
ATT_TILE = 512
MIX_TILE = 512
SEQ_PER_MIX_STEP = 64
POOL_TILE = 2560


def _params(sem):
    return pltpu.CompilerParams(dimension_semantics=sem, vmem_limit_bytes=VMEM_LIMIT)


def _resident(shape, index_map):
    return pl.BlockSpec(shape, index_map, pipeline_mode=pl.Buffered(1))


def _rms(x, g):
    return (x * lax.rsqrt(jnp.mean(x * x, axis=-1, keepdims=True) + RMS_EPS)) * g


def _softplus(x):
    return jnp.maximum(x, 0.0) + jnp.log1p(jnp.exp(-jnp.abs(x)))


def _dot(a, b):
    return jnp.dot(a, b, preferred_element_type=F32)


def _dot_nt(a, b):
    return lax.dot_general(a, b, (((1,), (1,)), ((), ())), preferred_element_type=F32)


def _prompt_only(n_p):
    return lambda i: (jnp.minimum(i, n_p - 1), 0)


def _sample_only(n_p):
    return lambda i: (jnp.maximum(i - n_p, 0), 0)


def _ffn_kernel(*refs, n_p, split_in, split_out):
    i = pl.program_id(0)
    n_x = 2 if split_in else 1
    g_ref, w1_ref, w2_ref = refs[n_x:n_x + 3]
    outs = refs[n_x + 3:]
    if split_in:
        x = jnp.where(i < n_p, refs[0][...], refs[1][...])
    else:
        x = refs[0][...]
    xn = _rms(x, g_ref[...]).astype(BF16)
    acc = jnp.zeros(x.shape, F32)
    for c in range(FFN_HIDDEN // FFN_CHUNK):
        lo = c * FFN_CHUNK
        gate = _dot(xn, w1_ref[:, lo:lo + FFN_CHUNK])
        up = _dot(xn, w1_ref[:, FFN_HIDDEN + lo:FFN_HIDDEN + lo + FFN_CHUNK])
        act = (gate * jax.nn.sigmoid(gate)) * up
        acc = acc + _dot(act.astype(BF16), w2_ref[lo:lo + FFN_CHUNK, :])
    y = x + 0.5 * acc
    if split_out:
        @pl.when(i < n_p)
        def _():
            outs[0][...] = y

        @pl.when(i >= n_p)
        def _():
            outs[1][...] = y
    else:
        outs[0][...] = y


def _ffn(xs, norm, w1, w2, layer, rows_p, rows_s, split_out):
    n_p = rows_p // ROW_TILE
    rows = rows_p + rows_s
    split_in = isinstance(xs, tuple)
    tile = (ROW_TILE, D_MODEL)
    pair = [pl.BlockSpec(tile, _prompt_only(n_p)), pl.BlockSpec(tile, _sample_only(n_p))]
    whole = [pl.BlockSpec(tile, lambda i: (i, 0))]
    if split_out:
        out_shape = [jax.ShapeDtypeStruct((rows_p, D_MODEL), F32), jax.ShapeDtypeStruct((rows_s, D_MODEL), F32)]
    else:
        out_shape = [jax.ShapeDtypeStruct((rows, D_MODEL), F32)]
    out = pl.pallas_call(
        functools.partial(_ffn_kernel, n_p=n_p, split_in=split_in, split_out=split_out),
        grid=(rows // ROW_TILE,),
        in_specs=(pair if split_in else whole) + [
            pl.BlockSpec((None, 1, D_MODEL), lambda i: (layer, 0, 0)),
            _resident((None, D_MODEL, 2 * FFN_HIDDEN), lambda i: (layer, 0, 0)),
            _resident((None, FFN_HIDDEN, D_MODEL), lambda i: (layer, 0, 0)),
        ],
        out_specs=pair if split_out else whole,
        out_shape=out_shape,
        compiler_params=_params(("arbitrary",)),
        name="ffn",
    )(*(xs if split_in else (xs,)), norm, w1, w2)
    return tuple(out) if split_out else out[0]


def _head_rms(x, gamma):
    m = x.shape[0]
    lane = lax.broadcasted_iota(jnp.int32, (m, LANES), 1)
    first = lane < HEAD_DIM
    outs = []
    for c in range(WIDTH_A // LANES):
        xc = x[:, c * LANES:(c + 1) * LANES]
        sq = xc * xc
        s_lo = jnp.sum(jnp.where(first, sq, 0.0), axis=-1, keepdims=True)
        s_hi = jnp.sum(jnp.where(first, 0.0, sq), axis=-1, keepdims=True)
        ms = jnp.where(first, s_lo, s_hi) * (1.0 / HEAD_DIM)
        outs.append((xc * lax.rsqrt(ms + RMS_EPS)) * gamma[:, c * LANES:(c + 1) * LANES])
    return jnp.concatenate(outs, axis=-1)


def _inproj_kernel(x_ref, g_ref, w_ref, bf_ref, qn_ref, kn_ref, *rest, n_p, n_alias):
    (qb_ref, kb_ref, vb_ref, qs_ref, ks_ref, vs_ref, lfs_ref,
     kt_ref, vt_ref, lft_ref, mix_ref, gate_ref) = rest[n_alias:]
    i = pl.program_id(0)
    xn = _rms(x_ref[...], g_ref[...]).astype(BF16)
    qkv = _dot(xn, w_ref[:, 0:P_MIX])
    q = _head_rms(qkv[:, 0:WIDTH_A], qn_ref[...]) * (HEAD_DIM ** -0.5 * LOG2E)
    k = _head_rms(qkv[:, WIDTH_A:2 * WIDTH_A], kn_ref[...])
    v = qkv[:, 2 * WIDTH_A:3 * WIDTH_A]
    qb_ref[...] = q.astype(BF16)
    kb_ref[...] = k.astype(BF16)
    vb_ref[...] = v.astype(BF16)
    mix_ref[...] = _dot(xn, w_ref[:, P_MIX:P_G])
    gate_ref[...] = jax.nn.sigmoid(_dot(xn, w_ref[:, P_G:P_F])).astype(BF16)
    f = _dot(xn, w_ref[:, P_F:P_END]) + bf_ref[...]
    lf = jnp.minimum(f, 0.0) - jnp.log1p(jnp.exp(-jnp.abs(f)))

    @pl.when(i < n_p)
    def _():
        kt_ref[...] = k.T
        vt_ref[...] = v.T
        lft_ref[...] = lf.T[0:N_HEADS, :]

    @pl.when(i >= n_p)
    def _():
        qs_ref[...] = q
        ks_ref[...] = k
        vs_ref[...] = v
        lfs_ref[...] = lf


def _inproj(x, norm, w, bf, qn, kn, layer, depth, batch, seq, rows_s, prev):
    rows = x.shape[0]
    n_p = batch * seq // ROW_TILE
    per_seq = seq // ROW_TILE
    row_spec = lambda w_: pl.BlockSpec((ROW_TILE, w_), lambda i: (i, 0))
    smp_spec = lambda w_: pl.BlockSpec((ROW_TILE, w_), _sample_only(n_p))
    vec_spec = lambda w_: pl.BlockSpec((None, 1, w_), lambda i: (layer, 0, 0))

    def state_spec(r):
        def im(i):
            j = jnp.minimum(i, n_p - 1)
            return (j // per_seq, layer, 0, j % per_seq)
        return pl.BlockSpec((None, None, r, ROW_TILE), im)

    n_alias = 0 if prev is None else 3
    in_specs = [row_spec(D_MODEL), vec_spec(D_MODEL), _resident((None, D_MODEL, P_END), lambda i: (layer, 0, 0)),
                vec_spec(LANES), vec_spec(WIDTH_A), vec_spec(WIDTH_A)]
    n_in = len(in_specs)
    in_specs += [pl.BlockSpec(memory_space=pl.ANY)] * n_alias
    sds = jax.ShapeDtypeStruct
    return pl.pallas_call(
        functools.partial(_inproj_kernel, n_p=n_p, n_alias=n_alias),
        grid=(rows // ROW_TILE,),
        in_specs=in_specs,
        out_specs=[
            row_spec(WIDTH_A), row_spec(WIDTH_A), row_spec(WIDTH_A),
            smp_spec(WIDTH_A), smp_spec(WIDTH_A), smp_spec(WIDTH_A), smp_spec(LANES),
            state_spec(WIDTH_A), state_spec(WIDTH_A), state_spec(N_HEADS),
            row_spec(W_MIX), row_spec(W_G),
        ],
        out_shape=[
            sds((rows, WIDTH_A), BF16), sds((rows, WIDTH_A), BF16), sds((rows, WIDTH_A), BF16),
            sds((rows_s, WIDTH_A), F32), sds((rows_s, WIDTH_A), F32), sds((rows_s, WIDTH_A), F32),
            sds((rows_s, LANES), F32),
            sds((batch, depth, WIDTH_A, seq), F32), sds((batch, depth, WIDTH_A, seq), F32),
            sds((batch, depth, N_HEADS, seq), F32),
            sds((rows, W_MIX), F32), sds((rows, W_G), BF16),
        ],
        input_output_aliases={n_in + a: 7 + a for a in range(n_alias)},
        compiler_params=_params(("arbitrary",)),
        name="inproj",
    )(x, norm, w, bf, qn, kn, *(prev or ()))


def _lane_prefix(x, n_valid):
    lane = lax.broadcasted_iota(jnp.int32, x.shape, 1)
    shift = 1
    while shift < n_valid:
        x = x + jnp.where(lane >= shift, pltpu.roll(x, shift, 1), 0.0)
        shift *= 2
    return x


def _cumsum_kernel(x_ref, o_ref):
    o_ref[...] = _lane_prefix(x_ref[...], x_ref.shape[1]) * LOG2E


def _cumsum(lft, layer, batch, seq):
    return pl.pallas_call(
        _cumsum_kernel,
        grid=(batch,),
        in_specs=[pl.BlockSpec((None, None, N_HEADS, seq), lambda b: (b, layer, 0, 0))],
        out_specs=pl.BlockSpec((N_HEADS, seq), lambda b: (0, b)),
        out_shape=jax.ShapeDtypeStruct((N_HEADS, batch * seq), F32),
        compiler_params=_params(("parallel",)),
        name="cumsum",
    )(lft)


def _pool_prefix_kernel(x_ref, o_ref):
    o_ref[...] = _lane_prefix(x_ref[...], PAGE) * LOG2E


def _pool_prefix(pool):
    rows = pool.shape[0]
    return pl.pallas_call(
        _pool_prefix_kernel,
        grid=(rows // POOL_TILE,),
        in_specs=[pl.BlockSpec((POOL_TILE, PAGE), lambda i: (i, 0))],
        out_specs=pl.BlockSpec((POOL_TILE, PAGE), lambda i: (i, 0)),
        out_shape=jax.ShapeDtypeStruct(pool.shape, F32),
        compiler_params=_params(("parallel",)),
        name="pool_prefix",
    )(pool)


def _attn_p_kernel(q_ref, k_ref, v_ref, c_ref, ct_ref, o_ref, m_ref, acc_ref, cq_ref, vx_ref):
    t = ATT_TILE
    hp = pl.program_id(1)
    qi = pl.program_id(2)
    lane = lax.broadcasted_iota(jnp.int32, (t, LANES), 1)
    first = lane < HEAD_DIM

    @pl.when(qi == 0)
    def _():
        v = v_ref[...]
        own = lax.broadcasted_iota(jnp.int32, v.shape, 1) < HEAD_DIM
        one = jnp.ones_like(v)
        vx_ref[0] = jnp.where(own, v, one)
        vx_ref[1] = jnp.where(own, one, v)

    q2 = q_ref[...]
    zero = jnp.zeros_like(q2)
    q_heads = (jnp.where(first, q2, zero), jnp.where(first, zero, q2))
    crow = c_ref[...]
    hlane = lax.broadcasted_iota(jnp.int32, crow.shape, 1)
    for e in range(2):
        cq = jnp.sum(jnp.where(hlane == 2 * hp + e, crow, 0.0), axis=-1, keepdims=True)
        cq_ref[e] = jnp.broadcast_to(cq, (t, LANES))
    m_ref[...] = jnp.full(m_ref.shape, NEG, F32)
    acc_ref[...] = jnp.zeros(acc_ref.shape, F32)
    row = lax.broadcasted_iota(jnp.int32, (t, t), 0)
    col = lax.broadcasted_iota(jnp.int32, (t, t), 1)
    causal = col <= row
    wide = lambda x: jnp.concatenate([x] * (t // LANES), axis=1)

    def step(j, masked):
        start = pl.multiple_of(j * t, t)
        kc = k_ref[pl.ds(start, t), :]
        for e in range(2):
            ck = ct_ref[pl.ds(2 * hp + e, 1), pl.ds(start, t)]
            u = _dot_nt(q_heads[e], kc) - ck
            if masked:
                u = jnp.where(causal, u, NEG)
            cq = cq_ref[e]
            m_prev = m_ref[e]
            m_new = jnp.maximum(m_prev, jnp.max(u, axis=-1, keepdims=True) + cq)
            alpha = jnp.exp2(m_prev - m_new)
            p = jnp.exp2(u + wide(cq - m_new))
            acc_ref[e] = alpha * acc_ref[e] + _dot(p.astype(BF16), vx_ref[e, pl.ds(start, t), :])
            m_ref[e] = m_new

    def body(j, carry):
        step(j, False)
        return carry

    lax.fori_loop(0, qi, body, 0)
    step(qi, True)
    a0 = acc_ref[0]
    a1 = acc_ref[1]
    o_ref[...] = jnp.where(first, a0 / pltpu.roll(a0, HEAD_DIM, 1), a1 / pltpu.roll(a1, HEAD_DIM, 1))


def _attn_p(qb, kb, vb, c_rows, ct, batch, seq):
    t = ATT_TILE
    nq = seq // t
    return pl.pallas_call(
        _attn_p_kernel,
        grid=(batch, WIDTH_A // LANES, nq),
        in_specs=[
            pl.BlockSpec((t, LANES), lambda b, h, i: (b * nq + i, h)),
            pl.BlockSpec((seq, LANES), lambda b, h, i: (b, h)),
            pl.BlockSpec((seq, LANES), lambda b, h, i: (b, h)),
            pl.BlockSpec((t, N_HEADS), lambda b, h, i: (b * nq + i, 0)),
            pl.BlockSpec((N_HEADS, seq), lambda b, h, i: (0, b)),
        ],
        out_specs=pl.BlockSpec((t, LANES), lambda b, h, i: (b * nq + i, h)),
        out_shape=jax.ShapeDtypeStruct((qb.shape[0], WIDTH_A), F32),
        scratch_shapes=[
            pltpu.VMEM((2, t, LANES), F32),
            pltpu.VMEM((2, t, LANES), F32),
            pltpu.VMEM((2, t, LANES), F32),
            pltpu.VMEM((2, seq, LANES), BF16),
        ],
        compiler_params=_params(("parallel", "parallel", "arbitrary")),
        name="attn_p",
    )(qb, kb, vb, c_rows, ct)


def _transpose_rows8(x):
    pad = jnp.concatenate([x, jnp.zeros((LANES - SUBLANES, LANES), F32)], axis=0)
    return pad.T[0:SUBLANES, :]


def _attn_s_kernel(pt_ref, q_ref, kn_ref, vn_ref, lfn_ref, *rest, n_pages, t_new):
    k_refs = rest[0:n_pages]
    v_refs = rest[n_pages:2 * n_pages]
    c_refs = rest[2 * n_pages:3 * n_pages]
    o_ref = rest[3 * n_pages + 1]
    rows = N_HEADS * t_new
    rhead = lax.broadcasted_iota(jnp.int32, (rows, WIDTH_A), 0) // t_new
    chead = lax.broadcasted_iota(jnp.int32, (rows, WIDTH_A), 1) // HEAD_DIM
    diag = rhead == chead
    q = q_ref[...]
    qbd = jnp.where(diag, jnp.concatenate([q] * N_HEADS, axis=0), 0.0).astype(BF16)

    def rep(ct):
        return jnp.broadcast_to(ct[:, None, :], (N_HEADS, t_new, LANES)).reshape(rows, LANES)

    def page(ref):
        return ref[...].reshape(WIDTH_A, PAGE).astype(BF16)

    carry = jnp.zeros((N_HEADS, 1), F32)
    cts = []
    for j in range(n_pages):
        within = c_refs[j][...]
        cts.append(within + carry)
        carry = carry + within[:, PAGE - 1:PAGE]
    lft_new = _transpose_rows8(lfn_ref[...])
    ct_new = _lane_prefix(lft_new, t_new) * LOG2E + carry
    lane = lax.broadcasted_iota(jnp.int32, (rows, LANES), 1)
    tq = lax.broadcasted_iota(jnp.int32, (rows, LANES), 0) % t_new
    ct_new_rep = rep(ct_new)
    cq = jnp.sum(jnp.where(lane == tq, ct_new_rep, 0.0), axis=-1, keepdims=True)

    u_blocks = [_dot(qbd, page(k_refs[j])) - rep(cts[j]) for j in range(n_pages)]
    zeros_pad = jnp.zeros((PAGE - t_new, WIDTH_A), F32)
    k_new = jnp.concatenate([kn_ref[...], zeros_pad], axis=0).astype(BF16)
    v_new = jnp.concatenate([vn_ref[...], zeros_pad], axis=0).astype(BF16)
    u_blocks.append(jnp.where(lane <= tq, _dot_nt(qbd, k_new) - ct_new_rep, NEG))

    mu = u_blocks[0]
    for u in u_blocks[1:]:
        mu = jnp.maximum(mu, u)
    m = jnp.max(mu, axis=-1, keepdims=True) + cq
    l = jnp.zeros((rows, 1), F32)
    acc = jnp.zeros((rows, WIDTH_A), F32)
    for j in range(n_pages + 1):
        p = jnp.exp2((u_blocks[j] + cq) - m)
        l = l + jnp.sum(p, axis=-1, keepdims=True)
        pb = p.astype(BF16)
        acc = acc + (_dot(pb, v_new) if j == n_pages else _dot_nt(pb, page(v_refs[j])))
    o = jnp.where(diag, acc / l, 0.0)
    out = o[0:t_new, :]
    for h in range(1, N_HEADS):
        out = out + o[h * t_new:(h + 1) * t_new, :]
    o_ref[...] = out


def _attn_s(page_table, q, k, v, lf, cache_k, cache_v, cache_c, oa, layer, row0, t_new):
    n_seq, n_pages = page_table.shape
    blk0 = row0 // t_new
    new_spec = lambda w_: pl.BlockSpec((t_new, w_), lambda n, pt: (n, 0))

    def page_spec(j, *dims):
        zeros = (0,) * len(dims)
        return pl.BlockSpec((None, None) + dims, lambda n, pt: (pt[n, j], layer) + zeros)

    in_specs = [new_spec(WIDTH_A), new_spec(WIDTH_A), new_spec(WIDTH_A), new_spec(LANES)]
    in_specs += [page_spec(j, N_HEADS, HEAD_DIM, PAGE) for j in range(n_pages)]
    in_specs += [page_spec(j, N_HEADS, HEAD_DIM, PAGE) for j in range(n_pages)]
    in_specs += [page_spec(j, N_HEADS, PAGE) for j in range(n_pages)]
    in_specs += [pl.BlockSpec(memory_space=pl.ANY)]
    grid_spec = pltpu.PrefetchScalarGridSpec(
        num_scalar_prefetch=1,
        grid=(n_seq,),
        in_specs=in_specs,
        out_specs=pl.BlockSpec((t_new, WIDTH_A), lambda n, pt: (blk0 + n, 0)),
    )
    return pl.pallas_call(
        functools.partial(_attn_s_kernel, n_pages=n_pages, t_new=t_new),
        grid_spec=grid_spec,
        out_shape=jax.ShapeDtypeStruct(oa.shape, oa.dtype),
        input_output_aliases={len(in_specs): 0},
        compiler_params=_params(("parallel",)),
        name="attn_s",
    )(page_table, q, k, v, lf, *([cache_k] * n_pages), *([cache_v] * n_pages), *([cache_c] * n_pages), oa)


def _lru_coeffs(xc, wa_ref, ba_ref, wx_ref, bx_ref, lam_ref):
    xb = xc.astype(BF16)
    r = jax.nn.sigmoid(_dot(xb, wa_ref[...]) + ba_ref[...])
    ig = jax.nn.sigmoid(_dot(xb, wx_ref[...]) + bx_ref[...])
    log_a = (-LRU_C * r) * _softplus(-lam_ref[...])
    a = jnp.exp(log_a)
    b = jnp.sqrt(1.0 - a * a) * (ig * xc)
    return a, b


def _mix_p_kernel(z_ref, cw_ref, cb_ref, wa_ref, ba_ref, wx_ref, bx_ref, lam_ref, sw_ref,
                  ob_ref, oc_ref, h_ref, lc_ref, scc_ref,
                  ubuf, pbuf, a_s, b_s, hcar):
    tt = MIX_TILE
    hist = SUBLANES
    i = pl.program_id(1)

    @pl.when(i == 0)
    def _():
        ubuf[0:hist, :] = jnp.zeros((hist, WIDTH_B), F32)
        pbuf[0:hist, :] = jnp.zeros((hist, WIDTH_C), F32)
        hcar[...] = jnp.zeros(hcar.shape, F32)

    u = z_ref[:, 0:WIDTH_B]
    ubuf[hist:hist + tt, :] = u
    xc = cw_ref[0:1, :] * ubuf[hist - 3:hist - 3 + tt, :]
    xc = xc + cw_ref[1:2, :] * ubuf[hist - 2:hist - 2 + tt, :]
    xc = xc + cw_ref[2:3, :] * ubuf[hist - 1:hist - 1 + tt, :]
    xc = xc + cw_ref[3:4, :] * u
    xc = xc + cb_ref[...]
    ubuf[0:hist, :] = ubuf[tt:tt + hist, :]
    lc_ref[...] = z_ref[tt - (CONV_B - 1):tt, 0:WIDTH_B]

    a, b = _lru_coeffs(xc, wa_ref, ba_ref, wx_ref, bx_ref, lam_ref)
    a_s[...] = a
    b_s[...] = b
    srow = lax.broadcasted_iota(jnp.int32, (SUBLANES, WIDTH_B), 0)

    def chunk(c, h):
        r0 = pl.multiple_of(c * SUBLANES, SUBLANES)
        aa = a_s[pl.ds(r0, SUBLANES), :]
        bb = b_s[pl.ds(r0, SUBLANES), :]
        for s in (1, 2, 4):
            keep = srow >= s
            a_sh = jnp.where(keep, pltpu.roll(aa, s, 0), 1.0)
            b_sh = jnp.where(keep, pltpu.roll(bb, s, 0), 0.0)
            bb = aa * b_sh + bb
            aa = aa * a_sh
        hc = aa * h + bb
        b_s[pl.ds(r0, SUBLANES), :] = hc
        return hc[SUBLANES - 1:SUBLANES, :]

    h_last = lax.fori_loop(0, tt // SUBLANES, chunk, hcar[...], unroll=4)
    hcar[...] = h_last
    h_ref[...] = h_last
    ob_ref[...] = b_s[...] * jax.nn.gelu(z_ref[:, WIDTH_B:2 * WIDTH_B])

    o3 = 2 * WIDTH_B
    pc = z_ref[:, o3 + WIDTH_C:o3 + 2 * WIDTH_C] * z_ref[:, o3 + 2 * WIDTH_C:o3 + 3 * WIDTH_C]
    pbuf[hist:hist + tt, :] = pc
    uc = sw_ref[0:1, :] * pbuf[hist - 2:hist - 2 + tt, :]
    uc = uc + sw_ref[1:2, :] * pbuf[hist - 1:hist - 1 + tt, :]
    uc = uc + sw_ref[2:3, :] * pc
    pbuf[0:hist, :] = pbuf[tt:tt + hist, :]
    scc_ref[...] = pbuf[hist + tt - (CONV_C - 1):hist + tt, :]
    oc_ref[...] = z_ref[:, o3:o3 + WIDTH_C] * uc


def _mix_weight_specs(layer):
    def im(*_):
        return (layer, 0, 0)

    return [
        pl.BlockSpec((None, CONV_B, WIDTH_B), im),
        pl.BlockSpec((None, 1, WIDTH_B), im),
        pl.BlockSpec((None, WIDTH_B, WIDTH_B), im),
        pl.BlockSpec((None, 1, WIDTH_B), im),
        pl.BlockSpec((None, WIDTH_B, WIDTH_B), im),
        pl.BlockSpec((None, 1, WIDTH_B), im),
        pl.BlockSpec((None, 1, WIDTH_B), im),
        pl.BlockSpec((None, CONV_C, WIDTH_C), im),
    ]


def _mix_p(zmix, weights, layer, batch, seq):
    tt = MIX_TILE
    nt = seq // tt
    rows = zmix.shape[0]
    return pl.pallas_call(
        _mix_p_kernel,
        grid=(batch, nt),
        in_specs=[pl.BlockSpec((tt, W_MIX), lambda b, i: (b * nt + i, 0))] + _mix_weight_specs(layer),
        out_specs=[
            pl.BlockSpec((tt, WIDTH_B), lambda b, i: (b * nt + i, 0)),
            pl.BlockSpec((tt, WIDTH_C), lambda b, i: (b * nt + i, 0)),
            pl.BlockSpec((None, 1, WIDTH_B), lambda b, i: (b, 0, 0)),
            pl.BlockSpec((None, CONV_B - 1, WIDTH_B), lambda b, i: (b, 0, 0)),
            pl.BlockSpec((None, CONV_C - 1, WIDTH_C), lambda b, i: (b, 0, 0)),
        ],
        out_shape=[
            jax.ShapeDtypeStruct((rows, WIDTH_B), F32),
            jax.ShapeDtypeStruct((rows, WIDTH_C), F32),
            jax.ShapeDtypeStruct((batch, 1, WIDTH_B), F32),
            jax.ShapeDtypeStruct((batch, CONV_B - 1, WIDTH_B), F32),
            jax.ShapeDtypeStruct((batch, CONV_C - 1, WIDTH_C), F32),
        ],
        scratch_shapes=[
            pltpu.VMEM((tt + SUBLANES, WIDTH_B), F32),
            pltpu.VMEM((tt + SUBLANES, WIDTH_C), F32),
            pltpu.VMEM((tt, WIDTH_B), F32),
            pltpu.VMEM((tt, WIDTH_B), F32),
            pltpu.VMEM((1, WIDTH_B), F32),
        ],
        compiler_params=_params(("arbitrary", "arbitrary")),
        name="mix_p",
    )(zmix, *weights)


def _mix_s_kernel(z_ref, h0_ref, lch_ref, sch_ref, cw_ref, cb_ref, wa_ref, ba_ref, wx_ref, bx_ref,
                  lam_ref, sw_ref, ob_alias, oc_alias, ob_ref, oc_ref, hs_ref, pc_ref, *, t_new):
    tt = z_ref.shape[0]
    o3 = 2 * WIDTH_B
    tpos = lax.broadcasted_iota(jnp.int32, (tt, WIDTH_B), 0) % t_new

    def back(x, hist, s):
        return jnp.where(tpos >= s, pltpu.roll(x, s, 0), pltpu.roll(hist, tt - (t_new - s), 0))

    u = z_ref[:, 0:WIDTH_B]
    lch = lch_ref[...]
    xc = cw_ref[0:1, :] * back(u, lch, 3)
    xc = xc + cw_ref[1:2, :] * back(u, lch, 2)
    xc = xc + cw_ref[2:3, :] * back(u, lch, 1)
    xc = xc + cw_ref[3:4, :] * u
    xc = xc + cb_ref[...]
    a, b = _lru_coeffs(xc, wa_ref, ba_ref, wx_ref, bx_ref, lam_ref)
    b = b + a * h0_ref[...]
    for s in (1, 2, 4):
        keep = tpos >= s
        a_sh = jnp.where(keep, pltpu.roll(a, s, 0), 1.0)
        b_sh = jnp.where(keep, pltpu.roll(b, s, 0), 0.0)
        b = a * b_sh + b
        a = a * a_sh
    hs_ref[...] = b
    ob_ref[...] = b * jax.nn.gelu(z_ref[:, WIDTH_B:2 * WIDTH_B])

    pc = z_ref[:, o3 + WIDTH_C:o3 + 2 * WIDTH_C] * z_ref[:, o3 + 2 * WIDTH_C:o3 + 3 * WIDTH_C]
    sch = sch_ref[...]
    uc = sw_ref[0:1, :] * back(pc, sch, 2)
    uc = uc + sw_ref[1:2, :] * back(pc, sch, 1)
    uc = uc + sw_ref[2:3, :] * pc
    pc_ref[...] = pc
    oc_ref[...] = z_ref[:, o3:o3 + WIDTH_C] * uc


def _mix_s(zmix, h0e, lch, sch, weights, ob, oc, layer, row0, n_seq, t_new):
    ns = SEQ_PER_MIX_STEP
    tt = ns * t_new
    blk0 = row0 // tt
    rows = n_seq * t_new
    row_spec = lambda w_: pl.BlockSpec((tt, w_), lambda i: (i, 0))
    all_spec = lambda w_: pl.BlockSpec((tt, w_), lambda i: (blk0 + i, 0))
    in_specs = [all_spec(W_MIX), row_spec(WIDTH_B), row_spec(WIDTH_B), row_spec(WIDTH_C)]
    in_specs += _mix_weight_specs(layer)
    in_specs += [pl.BlockSpec(memory_space=pl.ANY)] * 2
    return pl.pallas_call(
        functools.partial(_mix_s_kernel, t_new=t_new),
        grid=(n_seq // ns,),
        in_specs=in_specs,
        out_specs=[all_spec(WIDTH_B), all_spec(WIDTH_C), row_spec(WIDTH_B), row_spec(WIDTH_C)],
        out_shape=[
            jax.ShapeDtypeStruct(ob.shape, ob.dtype),
            jax.ShapeDtypeStruct(oc.shape, oc.dtype),
            jax.ShapeDtypeStruct((rows, WIDTH_B), F32),
            jax.ShapeDtypeStruct((rows, WIDTH_C), F32),
        ],
        input_output_aliases={len(in_specs) - 2: 0, len(in_specs) - 1: 1},
        compiler_params=_params(("parallel",)),
        name="mix_s",
    )(zmix, h0e, lch, sch, *weights, ob, oc)


def _merge_kernel(x_ref, oa_ref, ob_ref, oc_ref, g_ref, wb_ref, wo_ref, o_ref):
    merged = None
    lo = 0
    for b, (o_b, width) in enumerate(((oa_ref, WIDTH_A), (ob_ref, WIDTH_B), (oc_ref, WIDTH_C))):
        proj = _dot(o_b[...].astype(BF16), wb_ref[lo:lo + width, :])
        term = g_ref[:, b * D_MODEL:(b + 1) * D_MODEL].astype(F32) * proj
        merged = term if merged is None else merged + term
        lo += width
    o_ref[...] = x_ref[...] + _dot(merged.astype(BF16), wo_ref[...])


def _merge(x, oa, ob, oc, g, wb, wo, layer):
    rows = x.shape[0]
    row_spec = lambda w_: pl.BlockSpec((ROW_TILE, w_), lambda i: (i, 0))
    return pl.pallas_call(
        _merge_kernel,
        grid=(rows // ROW_TILE,),
        in_specs=[
            row_spec(D_MODEL), row_spec(WIDTH_A), row_spec(WIDTH_B), row_spec(WIDTH_C), row_spec(W_G),
            _resident((None, WIDTH_A + WIDTH_B + WIDTH_C, D_MODEL), lambda i: (layer, 0, 0)),
            _resident((None, D_MODEL, D_MODEL), lambda i: (layer, 0, 0)),
        ],
        out_specs=row_spec(D_MODEL),
        out_shape=jax.ShapeDtypeStruct((rows, D_MODEL), F32),
        compiler_params=_params(("parallel",)),
        name="merge",
    )(x, oa, ob, oc, g, wb, wo)


def kernel(x_prompt, x_sample, cache_k, cache_v, cache_logf, state_lru_h, state_lru_conv, state_sc_conv,
           page_table, ffn1_norm, ffn1_w_in, ffn1_w_out, mix_norm, w_in, b_forget, q_norm, k_norm,
           lru_conv_w, lru_conv_b, lru_wa, lru_ba, lru_wx, lru_bx, lru_lambda, sc_conv_w, w_branch, w_out,
           ffn2_norm, ffn2_w_in, ffn2_w_out):
    batch, seq, _ = x_prompt.shape
    n_seq, t_new, _ = x_sample.shape
    depth = w_in.shape[0]
    rows_p = batch * seq
    rows_s = n_seq * t_new

    vec = lambda a: a[:, None, :]
    w1a, w2a = ffn1_w_in.astype(BF16), ffn1_w_out.astype(BF16)
    w1b, w2b = ffn2_w_in.astype(BF16), ffn2_w_out.astype(BF16)
    w_in_p = jnp.concatenate(
        [w_in[:, :, :OFF_F], w_in[:, :, OFF_MIX:], w_in[:, :, OFF_F:OFF_MIX],
         jnp.zeros((depth, D_MODEL, LANES - N_HEADS), w_in.dtype)], axis=-1).astype(BF16)
    bf_p = jnp.pad(b_forget, ((0, 0), (0, LANES - N_HEADS)))[:, None, :]
    qn_p = vec(jnp.tile(q_norm, (1, N_HEADS)))
    kn_p = vec(jnp.tile(k_norm, (1, N_HEADS)))
    eye = jnp.eye(N_BLOCKS_B, dtype=lru_wa.dtype)
    dense = lambda w: jnp.einsum('lgij,gh->lgihj', w, eye).reshape(depth, WIDTH_B, WIDTH_B).astype(BF16)
    mix_weights = (lru_conv_w, vec(lru_conv_b), dense(lru_wa), vec(lru_ba), dense(lru_wx), vec(lru_bx),
                   vec(lru_lambda), sc_conv_w)
    wb_b, wo_b = w_branch.astype(BF16), w_out.astype(BF16)
    ck4 = jnp.transpose(cache_k, (0, 1, 3, 4, 2))
    cv4 = jnp.transpose(cache_v, (0, 1, 3, 4, 2))
    clft = jnp.swapaxes(cache_logf, 2, 3)
    cpool = _pool_prefix(clft.reshape(-1, PAGE)).reshape(clft.shape)

    def seq_rows(state, at_end):
        r = state.shape[1]
        pad = (t_new - r, 0) if at_end else (0, t_new - r)
        return jnp.pad(state, ((0, 0), pad, (0, 0))).reshape(rows_s, state.shape[-1])

    per_seq = lambda a: a.reshape(n_seq, t_new, a.shape[-1])
    x = (x_prompt.reshape(rows_p, D_MODEL), x_sample.reshape(rows_s, D_MODEL))
    prev = None
    outs_p = [[] for _ in range(3)]
    outs_s = [[] for _ in range(6)]
    for l in range(depth):
        x = _ffn(x, vec(ffn1_norm), w1a, w2a, l, rows_p, rows_s, False)
        (qb, kb, vb, qs, ks, vs, lfs, kt, vt, lft, zmix, gates) = _inproj(
            x, vec(mix_norm), w_in_p, bf_p, qn_p, kn_p, l, depth, batch, seq, rows_s, prev)
        prev = (kt, vt, lft)
        ct = _cumsum(lft, l, batch, seq)
        oa = _attn_p(qb, kb, vb, ct.T, ct, batch, seq)
        oa = _attn_s(page_table, qs, ks, vs, lfs, ck4, cv4, cpool, oa, l, rows_p, t_new)
        ob, oc, h_p, lc_p, sc_p = _mix_p(zmix, mix_weights, l, batch, seq)
        ob, oc, hs_s, pc_s = _mix_s(zmix, seq_rows(state_lru_h[:, l][:, None, :], False),
                                    seq_rows(state_lru_conv[:, l], True),
                                    seq_rows(state_sc_conv[:, l], True), mix_weights, ob, oc,
                                    l, rows_p, n_seq, t_new)
        x = _merge(x, oa, ob, oc, gates, wb_b, wo_b, l)
        x = _ffn(x, vec(ffn2_norm), w1b, w2b, l, rows_p, rows_s, l == depth - 1)
        for dst, val in zip(outs_p, (h_p[:, 0], lc_p, sc_p)):
            dst.append(val)
        for dst, val in zip(outs_s, (ks.reshape(n_seq, t_new, N_HEADS, HEAD_DIM),
                                     vs.reshape(n_seq, t_new, N_HEADS, HEAD_DIM),
                                     per_seq(lfs[:, :N_HEADS]),
                                     per_seq(hs_s)[:, -1],
                                     per_seq(zmix[rows_p:, :WIDTH_B])[:, t_new - (CONV_B - 1):],
                                     per_seq(pc_s)[:, t_new - (CONV_C - 1):])):
            dst.append(val)
    stack = lambda xs: jnp.stack(xs, axis=1)
    kt, vt, lft = prev
    heads = lambda a: jnp.transpose(a.reshape(batch, depth, N_HEADS, HEAD_DIM, seq), (0, 1, 4, 2, 3))
    y_p, y_s = x
    return ((y_p.reshape(batch, seq, D_MODEL), y_s.reshape(n_seq, t_new, D_MODEL),
             heads(kt), heads(vt), jnp.swapaxes(lft, 2, 3))
            + tuple(stack(o) for o in outs_p) + tuple(stack(o) for o in outs_s))
```

```python
import functools
import math

import jax
import jax.numpy as jnp
from jax import lax
from jax.experimental import pallas as pl
from jax.experimental.pallas import tpu as pltpu

F32 = jnp.float32
BF16 = jnp.bfloat16

D_MODEL = 1024
N_HEADS = 8
HEAD_DIM = 64
WIDTH_A = N_HEADS * HEAD_DIM
WIDTH_B = 512
N_BLOCKS_B = 8
CONV_B = 4
LRU_C = 8.0
WIDTH_C = 512
CONV_C = 3
N_BRANCH = 3
FFN_HIDDEN = 2816
RMS_EPS = 1e-6
PAGE = 128

LANES = 128
SUBLANES = 8
VMEM_LIMIT = 56 * 1024 * 1024
LOG2E = math.log2(math.e)

OFF_F = 3 * WIDTH_A
OFF_MIX = OFF_F + N_HEADS
W_MIX = 2 * WIDTH_B + 3 * WIDTH_C
OFF_G = OFF_MIX + W_MIX
W_G = N_BRANCH * D_MODEL

NEG = -1e30
ROW_TILE = 512
MXU_TILE = 256
FFN_CHUNKS = ((0, 5 * MXU_TILE), (5 * MXU_TILE, FFN_HIDDEN))
ATT_TILE = 512
ATT_COLS = 4
MIX_TILE = 512
SEQ_PER_MIX_STEP = 64
POOL_TILE = 2560


def _params(sem):
    return pltpu.CompilerParams(dimension_semantics=sem, vmem_limit_bytes=VMEM_LIMIT)


def _resident(shape, index_map):
    return pl.BlockSpec(shape, index_map, pipeline_mode=pl.Buffered(1))


def _rms(x, g):
    return (x * lax.rsqrt(jnp.mean(x * x, axis=-1, keepdims=True) + RMS_EPS)) * g


def _softplus(x):
    return jnp.maximum(x, 0.0) + jnp.log1p(jnp.exp(-jnp.abs(x)))


def _dot(a, b):
    return jnp.dot(a, b, preferred_element_type=F32)


def _dot_nt(a, b):
    return lax.dot_general(a, b, (((1,), (1,)), ((), ())), preferred_element_type=F32)


def _prompt_only(n_p):
    return lambda i: (jnp.minimum(i, n_p - 1), 0)


def _sample_only(n_p):
    return lambda i: (jnp.maximum(i - n_p, 0), 0)


def _ffn_kernel(*refs, n_p, split_in, split_out):
    i = pl.program_id(0)
    n_x = 2 if split_in else 1
    g_ref, w1_ref, w2_ref = refs[n_x:n_x + 3]
    outs = refs[n_x + 3:]
    if split_in:
        x = jnp.where(i < n_p, refs[0][...], refs[1][...])
    else:
        x = refs[0][...]
    xn = _rms(x, g_ref[...]).astype(BF16)
    acc = jnp.zeros(x.shape, F32)
    for lo, hi in FFN_CHUNKS:
        gate = _dot(xn, w1_ref[:, lo:hi])
        up = _dot(xn, w1_ref[:, FFN_HIDDEN + lo:FFN_HIDDEN + hi])
        act = (gate * jax.nn.sigmoid(gate)) * up
        acc = acc + _dot(act.astype(BF16), w2_ref[lo:hi, :])
    y = x + 0.5 * acc
    if split_out:
        @pl.when(i < n_p)
        def _():
            outs[0][...] = y

        @pl.when(i >= n_p)
        def _():
            outs[1][...] = y
    else:
        outs[0][...] = y


def _ffn(xs, norm, w1, w2, layer, rows_p, rows_s, split_out):
    n_p = rows_p // ROW_TILE
    rows = rows_p + rows_s
    split_in = isinstance(xs, tuple)
    tile = (ROW_TILE, D_MODEL)
    pair = [pl.BlockSpec(tile, _prompt_only(n_p)), pl.BlockSpec(tile, _sample_only(n_p))]
    whole = [pl.BlockSpec(tile, lambda i: (i, 0))]
    if split_out:
        out_shape = [jax.ShapeDtypeStruct((rows_p, D_MODEL), F32), jax.ShapeDtypeStruct((rows_s, D_MODEL), F32)]
    else:
        out_shape = [jax.ShapeDtypeStruct((rows, D_MODEL), F32)]
    out = pl.pallas_call(
        functools.partial(_ffn_kernel, n_p=n_p, split_in=split_in, split_out=split_out),
        grid=(rows // ROW_TILE,),
        in_specs=(pair if split_in else whole) + [
            pl.BlockSpec((None, 1, D_MODEL), lambda i: (layer, 0, 0)),
            _resident((None, D_MODEL, 2 * FFN_HIDDEN), lambda i: (layer, 0, 0)),
            _resident((None, FFN_HIDDEN, D_MODEL), lambda i: (layer, 0, 0)),
        ],
        out_specs=pair if split_out else whole,
        out_shape=out_shape,
        compiler_params=_params(("arbitrary",)),
        name="ffn",
    )(*(xs if split_in else (xs,)), norm, w1, w2)
    return tuple(out) if split_out else out[0]


def _head_rms(x, gamma):
    m = x.shape[0]
    lane = lax.broadcasted_iota(jnp.int32, (m, LANES), 1)
    first = lane < HEAD_DIM
    outs = []
    for c in range(WIDTH_A // LANES):
        xc = x[:, c * LANES:(c + 1) * LANES]
        sq = xc * xc
        s_lo = jnp.sum(jnp.where(first, sq, 0.0), axis=-1, keepdims=True)
        s_hi = jnp.sum(jnp.where(first, 0.0, sq), axis=-1, keepdims=True)
        ms = jnp.where(first, s_lo, s_hi) * (1.0 / HEAD_DIM)
        outs.append((xc * lax.rsqrt(ms + RMS_EPS)) * gamma[:, c * LANES:(c + 1) * LANES])
    return jnp.concatenate(outs, axis=-1)


def _inproj_kernel(x_ref, g_ref, wa_ref, wm_ref, wf_ref, bf_ref, qn_ref, kn_ref, *rest, n_p, n_alias):
    (qb_ref, kb_ref, vb_ref, qs_ref, ks_ref, vs_ref, lfs_ref,
     kt_ref, vt_ref, lft_ref, mix_ref, gate_ref) = rest[n_alias:]
    i = pl.program_id(0)
    xn = _rms(x_ref[...], g_ref[...]).astype(BF16)
    qkv = _dot(xn, wa_ref[...])
    q = _head_rms(qkv[:, 0:WIDTH_A], qn_ref[...]) * (HEAD_DIM ** -0.5 * LOG2E)
    k = _head_rms(qkv[:, WIDTH_A:2 * WIDTH_A], kn_ref[...])
    v = qkv[:, 2 * WIDTH_A:3 * WIDTH_A]
    qb_ref[...] = q.astype(BF16)
    kb_ref[...] = k.astype(BF16)
    vb_ref[...] = v.astype(BF16)
    mix_ref[...] = _dot(xn, wm_ref[:, 0:W_MIX])
    gate_ref[...] = jax.nn.sigmoid(_dot(xn, wm_ref[:, W_MIX:W_MIX + W_G])).astype(BF16)
    f = _dot(xn, wf_ref[...]) + bf_ref[...]
    lf = jnp.minimum(f, 0.0) - jnp.log1p(jnp.exp(-jnp.abs(f)))

    @pl.when(i < n_p)
    def _():
        kt_ref[...] = k.T
        vt_ref[...] = v.T
        lft_ref[...] = lf.T[0:N_HEADS, :]

    @pl.when(i >= n_p)
    def _():
        qs_ref[...] = q
        ks_ref[...] = k
        vs_ref[...] = v
        lfs_ref[...] = lf


def _inproj(x, norm, w, bf, qn, kn, layer, depth, batch, seq, rows_s, prev):
    rows = x.shape[0]
    n_p = batch * seq // ROW_TILE
    per_seq = seq // ROW_TILE
    row_spec = lambda w_: pl.BlockSpec((ROW_TILE, w_), lambda i: (i, 0))
    smp_spec = lambda w_: pl.BlockSpec((ROW_TILE, w_), _sample_only(n_p))
    vec_spec = lambda w_: pl.BlockSpec((None, 1, w_), lambda i: (layer, 0, 0))

    def state_spec(r):
        def im(i):
            j = jnp.minimum(i, n_p - 1)
            return (j // per_seq, layer, 0, j % per_seq)
        return pl.BlockSpec((None, None, r, ROW_TILE), im)

    n_alias = 0 if prev is None else 3
    w_spec = lambda w_: _resident((None, D_MODEL, w_), lambda i: (layer, 0, 0))
    in_specs = [row_spec(D_MODEL), vec_spec(D_MODEL), w_spec(3 * WIDTH_A), w_spec(W_MIX + W_G), w_spec(LANES),
                vec_spec(LANES), vec_spec(WIDTH_A), vec_spec(WIDTH_A)]
    n_in = len(in_specs)
    in_specs += [pl.BlockSpec(memory_space=pl.ANY)] * n_alias
    sds = jax.ShapeDtypeStruct
    return pl.pallas_call(
        functools.partial(_inproj_kernel, n_p=n_p, n_alias=n_alias),
        grid=(rows // ROW_TILE,),
        in_specs=in_specs,
        out_specs=[
            row_spec(WIDTH_A), row_spec(WIDTH_A), row_spec(WIDTH_A),
            smp_spec(WIDTH_A), smp_spec(WIDTH_A), smp_spec(WIDTH_A), smp_spec(LANES),
            state_spec(WIDTH_A), state_spec(WIDTH_A), state_spec(N_HEADS),
            row_spec(W_MIX), row_spec(W_G),
        ],
        out_shape=[
            sds((rows, WIDTH_A), BF16), sds((rows, WIDTH_A), BF16), sds((rows, WIDTH_A), BF16),
            sds((rows_s, WIDTH_A), F32), sds((rows_s, WIDTH_A), F32), sds((rows_s, WIDTH_A), F32),
            sds((rows_s, LANES), F32),
            sds((batch, depth, WIDTH_A, seq), F32), sds((batch, depth, WIDTH_A, seq), F32),
            sds((batch, depth, N_HEADS, seq), F32),
            sds((rows, W_MIX), F32), sds((rows, W_G), BF16),
        ],
        input_output_aliases={n_in + a: 7 + a for a in range(n_alias)},
        compiler_params=_params(("arbitrary",)),
        name="inproj",
    )(x, norm, *w, bf, qn, kn, *(prev or ()))


def _lane_prefix(x, n_valid):
    lane = lax.broadcasted_iota(jnp.int32, x.shape, 1)
    shift = 1
    while shift < n_valid:
        x = x + jnp.where(lane >= shift, pltpu.roll(x, shift, 1), 0.0)
        shift *= 2
    return x


def _cumsum_kernel(x_ref, o_ref):
    o_ref[...] = _lane_prefix(x_ref[...], x_ref.shape[1]) * LOG2E


def _cumsum(lft, layer, batch, seq):
    return pl.pallas_call(
        _cumsum_kernel,
        grid=(batch,),
        in_specs=[pl.BlockSpec((None, None, N_HEADS, seq), lambda b: (b, layer, 0, 0))],
        out_specs=pl.BlockSpec((N_HEADS, seq), lambda b: (0, b)),
        out_shape=jax.ShapeDtypeStruct((N_HEADS, batch * seq), F32),
        compiler_params=_params(("parallel",)),
        name="cumsum",
    )(lft)


def _pool_prefix_kernel(x_ref, o_ref):
    o_ref[...] = _lane_prefix(x_ref[...], PAGE) * LOG2E


def _pool_prefix(pool):
    rows = pool.shape[0]
    return pl.pallas_call(
        _pool_prefix_kernel,
        grid=(rows // POOL_TILE,),
        in_specs=[pl.BlockSpec((POOL_TILE, PAGE), lambda i: (i, 0))],
        out_specs=pl.BlockSpec((POOL_TILE, PAGE), lambda i: (i, 0)),
        out_shape=jax.ShapeDtypeStruct(pool.shape, F32),
        compiler_params=_params(("parallel",)),
        name="pool_prefix",
    )(pool)


def _attn_p_kernel(q_ref, k_ref, v_ref, c_ref, ct_ref, o_ref, m_ref, acc_ref, cq_ref, vx_ref):
    t = ATT_TILE
    n_heads = 2 * ATT_COLS
    head0 = pl.program_id(1) * n_heads
    qi = pl.program_id(2)
    lane = lax.broadcasted_iota(jnp.int32, (t, LANES), 1)
    first = lane < HEAD_DIM
    cols = lambda c: slice(c * LANES, (c + 1) * LANES)

    @pl.when(qi == 0)
    def _():
        own = lax.broadcasted_iota(jnp.int32, (v_ref.shape[0], LANES), 1) < HEAD_DIM
        for c in range(ATT_COLS):
            v = v_ref[:, cols(c)]
            one = jnp.ones_like(v)
            vx_ref[2 * c] = jnp.where(own, v, one)
            vx_ref[2 * c + 1] = jnp.where(own, one, v)

    q_heads = []
    for c in range(ATT_COLS):
        q2 = q_ref[:, cols(c)]
        zero = jnp.zeros_like(q2)
        q_heads += [jnp.where(first, q2, zero), jnp.where(first, zero, q2)]
    crow = c_ref[...]
    hlane = lax.broadcasted_iota(jnp.int32, crow.shape, 1)
    for e in range(n_heads):
        cq = jnp.sum(jnp.where(hlane == head0 + e, crow, 0.0), axis=-1, keepdims=True)
        cq_ref[e] = jnp.broadcast_to(cq, (t, LANES))
    m_ref[...] = jnp.full(m_ref.shape, NEG, F32)
    acc_ref[...] = jnp.zeros(acc_ref.shape, F32)
    row = lax.broadcasted_iota(jnp.int32, (t, t), 0)
    col = lax.broadcasted_iota(jnp.int32, (t, t), 1)
    causal = col <= row
    wide = lambda x: jnp.concatenate([x] * (t // LANES), axis=1)

    def step(j, masked):
        start = pl.multiple_of(j * t, t)
        for e in range(n_heads):
            kc = k_ref[pl.ds(start, t), cols(e // 2)]
            ck = ct_ref[pl.ds(head0 + e, 1), pl.ds(start, t)]
            u = _dot_nt(q_heads[e], kc) - ck
            if masked:
                u = jnp.where(causal, u, NEG)
            cq = cq_ref[e]
            m_prev = m_ref[e]
            m_new = jnp.maximum(m_prev, jnp.max(u, axis=-1, keepdims=True) + cq)
            alpha = jnp.exp2(m_prev - m_new)
            p = jnp.exp2(u + wide(cq - m_new))
            acc_ref[e] = alpha * acc_ref[e] + _dot(p.astype(BF16), vx_ref[e, pl.ds(start, t), :])
            m_ref[e] = m_new

    def body(j, carry):
        step(j, False)
        return carry

    lax.fori_loop(0, qi, body, 0)
    step(qi, True)
    for c in range(ATT_COLS):
        a0 = acc_ref[2 * c]
        a1 = acc_ref[2 * c + 1]
        o_ref[:, cols(c)] = jnp.where(first, a0 / pltpu.roll(a0, HEAD_DIM, 1), a1 / pltpu.roll(a1, HEAD_DIM, 1))


def _attn_p(qb, kb, vb, c_rows, ct, batch, seq):
    t = ATT_TILE
    nq = seq // t
    w = ATT_COLS * LANES
    return pl.pallas_call(
        _attn_p_kernel,
        grid=(batch, WIDTH_A // w, nq),
        in_specs=[
            pl.BlockSpec((t, w), lambda b, h, i: (b * nq + i, h)),
            pl.BlockSpec((seq, w), lambda b, h, i: (b, h)),
            pl.BlockSpec((seq, w), lambda b, h, i: (b, h)),
            pl.BlockSpec((t, N_HEADS), lambda b, h, i: (b * nq + i, 0)),
            pl.BlockSpec((N_HEADS, seq), lambda b, h, i: (0, b)),
        ],
        out_specs=pl.BlockSpec((t, w), lambda b, h, i: (b * nq + i, h)),
        out_shape=jax.ShapeDtypeStruct((qb.shape[0], WIDTH_A), F32),
        scratch_shapes=[
            pltpu.VMEM((2 * ATT_COLS, t, LANES), F32),
            pltpu.VMEM((2 * ATT_COLS, t, LANES), F32),
            pltpu.VMEM((2 * ATT_COLS, t, LANES), F32),
            pltpu.VMEM((2 * ATT_COLS, seq, LANES), BF16),
        ],
        compiler_params=_params(("parallel", "parallel", "arbitrary")),
        name="attn_p",
    )(qb, kb, vb, c_rows, ct)


def _transpose_rows8(x):
    pad = jnp.concatenate([x, jnp.zeros((LANES - SUBLANES, LANES), F32)], axis=0)
    return pad.T[0:SUBLANES, :]


def _attn_s_kernel(pt_ref, q_ref, kn_ref, vn_ref, lfn_ref, *rest, n_pages, t_new):
    k_refs = rest[0:n_pages]
    v_refs = rest[n_pages:2 * n_pages]
    c_refs = rest[2 * n_pages:3 * n_pages]
    o_ref = rest[3 * n_pages + 1]
    rows = N_HEADS * t_new
    rhead = lax.broadcasted_iota(jnp.int32, (rows, WIDTH_A), 0) // t_new
    chead = lax.broadcasted_iota(jnp.int32, (rows, WIDTH_A), 1) // HEAD_DIM
    diag = rhead == chead
    q = q_ref[...]
    qbd = jnp.where(diag, jnp.concatenate([q] * N_HEADS, axis=0), 0.0).astype(BF16)

    def rep(ct):
        return jnp.broadcast_to(ct[:, None, :], (N_HEADS, t_new, LANES)).reshape(rows, LANES)

    def page(ref):
        return ref[...].reshape(WIDTH_A, PAGE).astype(BF16)

    carry = jnp.zeros((N_HEADS, 1), F32)
    cts = []
    for j in range(n_pages):
        within = c_refs[j][...]
        cts.append(within + carry)
        carry = carry + within[:, PAGE - 1:PAGE]
    lft_new = _transpose_rows8(lfn_ref[...])
    ct_new = _lane_prefix(lft_new, t_new) * LOG2E + carry
    lane = lax.broadcasted_iota(jnp.int32, (rows, LANES), 1)
    tq = lax.broadcasted_iota(jnp.int32, (rows, LANES), 0) % t_new
    ct_new_rep = rep(ct_new)
    cq = jnp.sum(jnp.where(lane == tq, ct_new_rep, 0.0), axis=-1, keepdims=True)

    n_past = n_pages * PAGE
    k_past = jnp.concatenate([page(r) for r in k_refs], axis=1)
    v_past = jnp.concatenate([page(r) for r in v_refs], axis=1)
    ct_past = jnp.concatenate(cts, axis=1)
    ct_past_rep = jnp.broadcast_to(ct_past[:, None, :], (N_HEADS, t_new, n_past)).reshape(rows, n_past)
    u_past = _dot(qbd, k_past) - ct_past_rep
    zeros_pad = jnp.zeros((PAGE - t_new, WIDTH_A), F32)
    k_new = jnp.concatenate([kn_ref[...], zeros_pad], axis=0).astype(BF16)
    v_new = jnp.concatenate([vn_ref[...], zeros_pad], axis=0).astype(BF16)
    u_new = jnp.where(lane <= tq, _dot_nt(qbd, k_new) - ct_new_rep, NEG)

    m = jnp.maximum(jnp.max(u_past, axis=-1, keepdims=True), jnp.max(u_new, axis=-1, keepdims=True)) + cq
    p_past = jnp.exp2((u_past + cq) - m)
    p_new = jnp.exp2((u_new + cq) - m)
    l = jnp.sum(p_past, axis=-1, keepdims=True) + jnp.sum(p_new, axis=-1, keepdims=True)
    acc = _dot_nt(p_past.astype(BF16), v_past) + _dot(p_new.astype(BF16), v_new)
    o = jnp.where(diag, acc / l, 0.0)
    out = o[0:t_new, :]
    for h in range(1, N_HEADS):
        out = out + o[h * t_new:(h + 1) * t_new, :]
    o_ref[...] = out


def _attn_s(page_table, q, k, v, lf, cache_k, cache_v, cache_c, oa, layer, row0, t_new):
    n_seq, n_pages = page_table.shape
    blk0 = row0 // t_new
    new_spec = lambda w_: pl.BlockSpec((t_new, w_), lambda n, pt: (n, 0))

    def page_spec(j, *dims):
        zeros = (0,) * len(dims)
        return pl.BlockSpec((None, None) + dims, lambda n, pt: (pt[n, j], layer) + zeros)

    in_specs = [new_spec(WIDTH_A), new_spec(WIDTH_A), new_spec(WIDTH_A), new_spec(LANES)]
    in_specs += [page_spec(j, N_HEADS, HEAD_DIM, PAGE) for j in range(n_pages)]
    in_specs += [page_spec(j, N_HEADS, HEAD_DIM, PAGE) for j in range(n_pages)]
    in_specs += [page_spec(j, N_HEADS, PAGE) for j in range(n_pages)]
    in_specs += [pl.BlockSpec(memory_space=pl.ANY)]
    grid_spec = pltpu.PrefetchScalarGridSpec(
        num_scalar_prefetch=1,
        grid=(n_seq,),
        in_specs=in_specs,
        out_specs=pl.BlockSpec((t_new, WIDTH_A), lambda n, pt: (blk0 + n, 0)),
    )
    return pl.pallas_call(
        functools.partial(_attn_s_kernel, n_pages=n_pages, t_new=t_new),
        grid_spec=grid_spec,
        out_shape=jax.ShapeDtypeStruct(oa.shape, oa.dtype),
        input_output_aliases={len(in_specs): 0},
        compiler_params=_params(("parallel",)),
        name="attn_s",
    )(page_table, q, k, v, lf, *([cache_k] * n_pages), *([cache_v] * n_pages), *([cache_c] * n_pages), oa)


def _lru_coeffs(xc, wa_ref, ba_ref, wx_ref, bx_ref, lam_ref):
    xb = xc.astype(BF16)
    r = jax.nn.sigmoid(_dot(xb, wa_ref[...]) + ba_ref[...])
    ig = jax.nn.sigmoid(_dot(xb, wx_ref[...]) + bx_ref[...])
    log_a = (-LRU_C * r) * _softplus(-lam_ref[...])
    a = jnp.exp(log_a)
    b = jnp.sqrt(1.0 - a * a) * (ig * xc)
    return a, b


def _mix_p_kernel(z_ref, cw_ref, cb_ref, wa_ref, ba_ref, wx_ref, bx_ref, lam_ref, sw_ref,
                  ob_ref, oc_ref, h_ref, lc_ref, scc_ref,
                  ubuf, pbuf, a_s, b_s, hcar):
    tt = MIX_TILE
    hist = SUBLANES
    i = pl.program_id(1)

    @pl.when(i == 0)
    def _():
        ubuf[0:hist, :] = jnp.zeros((hist, WIDTH_B), F32)
        pbuf[0:hist, :] = jnp.zeros((hist, WIDTH_C), F32)
        hcar[...] = jnp.zeros(hcar.shape, F32)

    u = z_ref[:, 0:WIDTH_B]
    ubuf[hist:hist + tt, :] = u
    xc = cw_ref[0:1, :] * ubuf[hist - 3:hist - 3 + tt, :]
    xc = xc + cw_ref[1:2, :] * ubuf[hist - 2:hist - 2 + tt, :]
    xc = xc + cw_ref[2:3, :] * ubuf[hist - 1:hist - 1 + tt, :]
    xc = xc + cw_ref[3:4, :] * u
    xc = xc + cb_ref[...]
    ubuf[0:hist, :] = ubuf[tt:tt + hist, :]
    lc_ref[...] = z_ref[tt - (CONV_B - 1):tt, 0:WIDTH_B]

    a, b = _lru_coeffs(xc, wa_ref, ba_ref, wx_ref, bx_ref, lam_ref)
    a_s[...] = a
    b_s[...] = b
    srow = lax.broadcasted_iota(jnp.int32, (SUBLANES, WIDTH_B), 0)

    def chunk(c, h):
        r0 = pl.multiple_of(c * SUBLANES, SUBLANES)
        aa = a_s[pl.ds(r0, SUBLANES), :]
        bb = b_s[pl.ds(r0, SUBLANES), :]
        for s in (1, 2, 4):
            keep = srow >= s
            a_sh = jnp.where(keep, pltpu.roll(aa, s, 0), 1.0)
            b_sh = jnp.where(keep, pltpu.roll(bb, s, 0), 0.0)
            bb = aa * b_sh + bb
            aa = aa * a_sh
        hc = aa * h + bb
        b_s[pl.ds(r0, SUBLANES), :] = hc
        return hc[SUBLANES - 1:SUBLANES, :]

    h_last = lax.fori_loop(0, tt // SUBLANES, chunk, hcar[...], unroll=4)
    hcar[...] = h_last
    h_ref[...] = h_last
    ob_ref[...] = b_s[...] * jax.nn.gelu(z_ref[:, WIDTH_B:2 * WIDTH_B])

    o3 = 2 * WIDTH_B
    pc = z_ref[:, o3 + WIDTH_C:o3 + 2 * WIDTH_C] * z_ref[:, o3 + 2 * WIDTH_C:o3 + 3 * WIDTH_C]
    pbuf[hist:hist + tt, :] = pc
    uc = sw_ref[0:1, :] * pbuf[hist - 2:hist - 2 + tt, :]
    uc = uc + sw_ref[1:2, :] * pbuf[hist - 1:hist - 1 + tt, :]
    uc = uc + sw_ref[2:3, :] * pc
    pbuf[0:hist, :] = pbuf[tt:tt + hist, :]
    scc_ref[...] = pbuf[hist + tt - (CONV_C - 1):hist + tt, :]
    oc_ref[...] = z_ref[:, o3:o3 + WIDTH_C] * uc


def _mix_weight_specs(layer):
    def im(*_):
        return (layer, 0, 0)

    return [
        pl.BlockSpec((None, CONV_B, WIDTH_B), im),
        pl.BlockSpec((None, 1, WIDTH_B), im),
        pl.BlockSpec((None, WIDTH_B, WIDTH_B), im),
        pl.BlockSpec((None, 1, WIDTH_B), im),
        pl.BlockSpec((None, WIDTH_B, WIDTH_B), im),
        pl.BlockSpec((None, 1, WIDTH_B), im),
        pl.BlockSpec((None, 1, WIDTH_B), im),
        pl.BlockSpec((None, CONV_C, WIDTH_C), im),
    ]


def _mix_p(zmix, weights, layer, batch, seq):
    tt = MIX_TILE
    nt = seq // tt
    rows = zmix.shape[0]
    return pl.pallas_call(
        _mix_p_kernel,
        grid=(batch, nt),
        in_specs=[pl.BlockSpec((tt, W_MIX), lambda b, i: (b * nt + i, 0))] + _mix_weight_specs(layer),
        out_specs=[
            pl.BlockSpec((tt, WIDTH_B), lambda b, i: (b * nt + i, 0)),
            pl.BlockSpec((tt, WIDTH_C), lambda b, i: (b * nt + i, 0)),
            pl.BlockSpec((None, 1, WIDTH_B), lambda b, i: (b, 0, 0)),
            pl.BlockSpec((None, CONV_B - 1, WIDTH_B), lambda b, i: (b, 0, 0)),
            pl.BlockSpec((None, CONV_C - 1, WIDTH_C), lambda b, i: (b, 0, 0)),
        ],
        out_shape=[
            jax.ShapeDtypeStruct((rows, WIDTH_B), F32),
            jax.ShapeDtypeStruct((rows, WIDTH_C), F32),
            jax.ShapeDtypeStruct((batch, 1, WIDTH_B), F32),
            jax.ShapeDtypeStruct((batch, CONV_B - 1, WIDTH_B), F32),
            jax.ShapeDtypeStruct((batch, CONV_C - 1, WIDTH_C), F32),
        ],
        scratch_shapes=[
            pltpu.VMEM((tt + SUBLANES, WIDTH_B), F32),
            pltpu.VMEM((tt + SUBLANES, WIDTH_C), F32),
            pltpu.VMEM((tt, WIDTH_B), F32),
            pltpu.VMEM((tt, WIDTH_B), F32),
            pltpu.VMEM((1, WIDTH_B), F32),
        ],
        compiler_params=_params(("arbitrary", "arbitrary")),
        name="mix_p",
    )(zmix, *weights)


def _mix_s_kernel(z_ref, h0_ref, lch_ref, sch_ref, cw_ref, cb_ref, wa_ref, ba_ref, wx_ref, bx_ref,
                  lam_ref, sw_ref, ob_alias, oc_alias, ob_ref, oc_ref, hs_ref, pc_ref, *, t_new):
    tt = z_ref.shape[0]
    o3 = 2 * WIDTH_B
    tpos = lax.broadcasted_iota(jnp.int32, (tt, WIDTH_B), 0) % t_new

    def back(x, hist, s):
        return jnp.where(tpos >= s, pltpu.roll(x, s, 0), pltpu.roll(hist, tt - (t_new - s), 0))

    u = z_ref[:, 0:WIDTH_B]
    lch = lch_ref[...]
    xc = cw_ref[0:1, :] * back(u, lch, 3)
    xc = xc + cw_ref[1:2, :] * back(u, lch, 2)
    xc = xc + cw_ref[2:3, :] * back(u, lch, 1)
    xc = xc + cw_ref[3:4, :] * u
    xc = xc + cb_ref[...]
    a, b = _lru_coeffs(xc, wa_ref, ba_ref, wx_ref, bx_ref, lam_ref)
    b = b + a * h0_ref[...]
    for s in (1, 2, 4):
        keep = tpos >= s
        a_sh = jnp.where(keep, pltpu.roll(a, s, 0), 1.0)
        b_sh = jnp.where(keep, pltpu.roll(b, s, 0), 0.0)
        b = a * b_sh + b
        a = a * a_sh
    hs_ref[...] = b
    ob_ref[...] = b * jax.nn.gelu(z_ref[:, WIDTH_B:2 * WIDTH_B])

    pc = z_ref[:, o3 + WIDTH_C:o3 + 2 * WIDTH_C] * z_ref[:, o3 + 2 * WIDTH_C:o3 + 3 * WIDTH_C]
    sch = sch_ref[...]
    uc = sw_ref[0:1, :] * back(pc, sch, 2)
    uc = uc + sw_ref[1:2, :] * back(pc, sch, 1)
    uc = uc + sw_ref[2:3, :] * pc
    pc_ref[...] = pc
    oc_ref[...] = z_ref[:, o3:o3 + WIDTH_C] * uc


def _mix_s(zmix, h0e, lch, sch, weights, ob, oc, layer, row0, n_seq, t_new):
    ns = SEQ_PER_MIX_STEP
    tt = ns * t_new
    blk0 = row0 // tt
    rows = n_seq * t_new
    row_spec = lambda w_: pl.BlockSpec((tt, w_), lambda i: (i, 0))
    all_spec = lambda w_: pl.BlockSpec((tt, w_), lambda i: (blk0 + i, 0))
    in_specs = [all_spec(W_MIX), row_spec(WIDTH_B), row_spec(WIDTH_B), row_spec(WIDTH_C)]
    in_specs += _mix_weight_specs(layer)
    in_specs += [pl.BlockSpec(memory_space=pl.ANY)] * 2
    return pl.pallas_call(
        functools.partial(_mix_s_kernel, t_new=t_new),
        grid=(n_seq // ns,),
        in_specs=in_specs,
        out_specs=[all_spec(WIDTH_B), all_spec(WIDTH_C), row_spec(WIDTH_B), row_spec(WIDTH_C)],
        out_shape=[
            jax.ShapeDtypeStruct(ob.shape, ob.dtype),
            jax.ShapeDtypeStruct(oc.shape, oc.dtype),
            jax.ShapeDtypeStruct((rows, WIDTH_B), F32),
            jax.ShapeDtypeStruct((rows, WIDTH_C), F32),
        ],
        input_output_aliases={len(in_specs) - 2: 0, len(in_specs) - 1: 1},
        compiler_params=_params(("parallel",)),
        name="mix_s",
    )(zmix, h0e, lch, sch, *weights, ob, oc)


def _merge_kernel(x_ref, oa_ref, ob_ref, oc_ref, g_ref, wb_ref, wo_ref, o_ref):
    merged = None
    lo = 0
    for b, (o_b, width) in enumerate(((oa_ref, WIDTH_A), (ob_ref, WIDTH_B), (oc_ref, WIDTH_C))):
        proj = _dot(o_b[...].astype(BF16), wb_ref[lo:lo + width, :])
        term = g_ref[:, b * D_MODEL:(b + 1) * D_MODEL].astype(F32) * proj
        merged = term if merged is None else merged + term
        lo += width
    o_ref[...] = x_ref[...] + _dot(merged.astype(BF16), wo_ref[...])


def _merge(x, oa, ob, oc, g, wb, wo, layer):
    rows = x.shape[0]
    row_spec = lambda w_: pl.BlockSpec((ROW_TILE, w_), lambda i: (i, 0))
    return pl.pallas_call(
        _merge_kernel,
        grid=(rows // ROW_TILE,),
        in_specs=[
            row_spec(D_MODEL), row_spec(WIDTH_A), row_spec(WIDTH_B), row_spec(WIDTH_C), row_spec(W_G),
            _resident((None, WIDTH_A + WIDTH_B + WIDTH_C, D_MODEL), lambda i: (layer, 0, 0)),
            _resident((None, D_MODEL, D_MODEL), lambda i: (layer, 0, 0)),
        ],
        out_specs=row_spec(D_MODEL),
        out_shape=jax.ShapeDtypeStruct((rows, D_MODEL), F32),
        compiler_params=_params(("parallel",)),
        name="merge",
    )(x, oa, ob, oc, g, wb, wo)


def kernel(x_prompt, x_sample, cache_k, cache_v, cache_logf, state_lru_h, state_lru_conv, state_sc_conv,
           page_table, ffn1_norm, ffn1_w_in, ffn1_w_out, mix_norm, w_in, b_forget, q_norm, k_norm,
           lru_conv_w, lru_conv_b, lru_wa, lru_ba, lru_wx, lru_bx, lru_lambda, sc_conv_w, w_branch, w_out,
           ffn2_norm, ffn2_w_in, ffn2_w_out):
    batch, seq, _ = x_prompt.shape
    n_seq, t_new, _ = x_sample.shape
    depth = w_in.shape[0]
    rows_p = batch * seq
    rows_s = n_seq * t_new

    vec = lambda a: a[:, None, :]
    w1a, w2a = ffn1_w_in.astype(BF16), ffn1_w_out.astype(BF16)
    w1b, w2b = ffn2_w_in.astype(BF16), ffn2_w_out.astype(BF16)
    w_in_p = (w_in[:, :, :OFF_F].astype(BF16), w_in[:, :, OFF_MIX:].astype(BF16),
              jnp.pad(w_in[:, :, OFF_F:OFF_MIX], ((0, 0), (0, 0), (0, LANES - N_HEADS))).astype(BF16))
    bf_p = jnp.pad(b_forget, ((0, 0), (0, LANES - N_HEADS)))[:, None, :]
    qn_p = vec(jnp.tile(q_norm, (1, N_HEADS)))
    kn_p = vec(jnp.tile(k_norm, (1, N_HEADS)))
    eye = jnp.eye(N_BLOCKS_B, dtype=lru_wa.dtype)
    dense = lambda w: jnp.einsum('lgij,gh->lgihj', w, eye).reshape(depth, WIDTH_B, WIDTH_B).astype(BF16)
    mix_weights = (lru_conv_w, vec(lru_conv_b), dense(lru_wa), vec(lru_ba), dense(lru_wx), vec(lru_bx),
                   vec(lru_lambda), sc_conv_w)
    wb_b, wo_b = w_branch.astype(BF16), w_out.astype(BF16)
    ck4 = jnp.transpose(cache_k, (0, 1, 3, 4, 2))
    cv4 = jnp.transpose(cache_v, (0, 1, 3, 4, 2))
    clft = jnp.swapaxes(cache_logf, 2, 3)
    cpool = _pool_prefix(clft.reshape(-1, PAGE)).reshape(clft.shape)

    def seq_rows(state, at_end):
        r = state.shape[1]
        pad = (t_new - r, 0) if at_end else (0, t_new - r)
        return jnp.pad(state, ((0, 0), pad, (0, 0))).reshape(rows_s, state.shape[-1])

    per_seq = lambda a: a.reshape(n_seq, t_new, a.shape[-1])
    x = (x_prompt.reshape(rows_p, D_MODEL), x_sample.reshape(rows_s, D_MODEL))
    prev = None
    outs_p = [[] for _ in range(3)]
    outs_s = [[] for _ in range(6)]
    for l in range(depth):
        x = _ffn(x, vec(ffn1_norm), w1a, w2a, l, rows_p, rows_s, False)
        (qb, kb, vb, qs, ks, vs, lfs, kt, vt, lft, zmix, gates) = _inproj(
            x, vec(mix_norm), w_in_p, bf_p, qn_p, kn_p, l, depth, batch, seq, rows_s, prev)
        prev = (kt, vt, lft)
        ct = _cumsum(lft, l, batch, seq)
        oa = _attn_p(qb, kb, vb, ct.T, ct, batch, seq)
        oa = _attn_s(page_table, qs, ks, vs, lfs, ck4, cv4, cpool, oa, l, rows_p, t_new)
        ob, oc, h_p, lc_p, sc_p = _mix_p(zmix, mix_weights, l, batch, seq)
        ob, oc, hs_s, pc_s = _mix_s(zmix, seq_rows(state_lru_h[:, l][:, None, :], False),
                                    seq_rows(state_lru_conv[:, l], True),
                                    seq_rows(state_sc_conv[:, l], True), mix_weights, ob, oc,
                                    l, rows_p, n_seq, t_new)
        x = _merge(x, oa, ob, oc, gates, wb_b, wo_b, l)
        x = _ffn(x, vec(ffn2_norm), w1b, w2b, l, rows_p, rows_s, l == depth - 1)
        for dst, val in zip(outs_p, (h_p[:, 0], lc_p, sc_p)):
            dst.append(val)
        for dst, val in zip(outs_s, (ks.reshape(n_seq, t_new, N_HEADS, HEAD_DIM),
                                     vs.reshape(n_seq, t_new, N_HEADS, HEAD_DIM),
                                     per_seq(lfs[:, :N_HEADS]),
                                     per_seq(hs_s)[:, -1],
                                     per_seq(zmix[rows_p:, :WIDTH_B])[:, t_new - (CONV_B - 1):],
                                     per_seq(pc_s)[:, t_new - (CONV_C - 1):])):
            dst.append(val)
    stack = lambda xs: jnp.stack(xs, axis=1)
    kt, vt, lft = prev
    heads = lambda a: jnp.transpose(a.reshape(batch, depth, N_HEADS, HEAD_DIM, seq), (0, 1, 4, 2, 3))
    y_p, y_s = x
    return ((y_p.reshape(batch, seq, D_MODEL), y_s.reshape(n_seq, t_new, D_MODEL),
             heads(kt), heads(vt), jnp.swapaxes(lft, 2, 3))
            + tuple(stack(o) for o in outs_p) + tuple(stack(o) for o in outs_s))
```

```python
import functools
import math

import jax
import jax.numpy as jnp
from jax import lax
from jax.experimental import pallas as pl
from jax.experimental.pallas import tpu as pltpu

F32 = jnp.float32
BF16 = jnp.bfloat16

D_MODEL = 1024
N_HEADS = 8
HEAD_DIM = 64
WIDTH_A = N_HEADS * HEAD_DIM
WIDTH_B = 512
N_BLOCKS_B = 8
CONV_B = 4
LRU_C = 8.0
WIDTH_C = 512
CONV_C = 3
N_BRANCH = 3
FFN_HIDDEN = 2816
RMS_EPS = 1e-6
PAGE = 128

LANES = 128
SUBLANES = 8
VMEM_LIMIT = 56 * 1024 * 1024
LOG2E = math.log2(math.e)

OFF_F = 3 * WIDTH_A
OFF_MIX = OFF_F + N_HEADS
W_MIX = 2 * WIDTH_B + 3 * WIDTH_C
OFF_G = OFF_MIX + W_MIX
W_G = N_BRANCH * D_MODEL

NEG = -1e30
ROW_TILE = 512
MXU_TILE = 256
FFN_CHUNKS = ((0, 5 * MXU_TILE), (5 * MXU_TILE, FFN_HIDDEN))
ATT_TILE = 512
ATT_COLS = 4
MIX_TILE = 512
SEQ_PER_MIX_STEP = 64
POOL_TILE = 2560
PAGE_AHEAD = 2
PAGE_SLOTS = PAGE_AHEAD + 1


def _params(sem):
    return pltpu.CompilerParams(dimension_semantics=sem, vmem_limit_bytes=VMEM_LIMIT)


def _resident(shape, index_map):
    return pl.BlockSpec(shape, index_map, pipeline_mode=pl.Buffered(1))


def _rms(x, g):
    return (x * lax.rsqrt(jnp.mean(x * x, axis=-1, keepdims=True) + RMS_EPS)) * g


def _softplus(x):
    return jnp.maximum(x, 0.0) + jnp.log1p(jnp.exp(-jnp.abs(x)))


def _dot(a, b):
    return jnp.dot(a, b, preferred_element_type=F32)


def _dot_nt(a, b):
    return lax.dot_general(a, b, (((1,), (1,)), ((), ())), preferred_element_type=F32)


def _prompt_only(n_p):
    return lambda i: (jnp.minimum(i, n_p - 1), 0)


def _sample_only(n_p):
    return lambda i: (jnp.maximum(i - n_p, 0), 0)


def _ffn_kernel(*refs, n_p, split_in, split_out):
    i = pl.program_id(0)
    n_x = 2 if split_in else 1
    g_ref, w1_ref, w2_ref = refs[n_x:n_x + 3]
    outs = refs[n_x + 3:]
    if split_in:
        x = jnp.where(i < n_p, refs[0][...], refs[1][...])
    else:
        x = refs[0][...]
    xn = _rms(x, g_ref[...]).astype(BF16)
    acc = jnp.zeros(x.shape, F32)
    for lo, hi in FFN_CHUNKS:
        gate = _dot(xn, w1_ref[:, lo:hi])
        up = _dot(xn, w1_ref[:, FFN_HIDDEN + lo:FFN_HIDDEN + hi])
        act = (gate * jax.nn.sigmoid(gate)) * up
        acc = acc + _dot(act.astype(BF16), w2_ref[lo:hi, :])
    y = x + 0.5 * acc
    if split_out:
        @pl.when(i < n_p)
        def _():
            outs[0][...] = y

        @pl.when(i >= n_p)
        def _():
            outs[1][...] = y
    else:
        outs[0][...] = y


def _ffn(xs, norm, w1, w2, layer, rows_p, rows_s, split_out):
    n_p = rows_p // ROW_TILE
    rows = rows_p + rows_s
    split_in = isinstance(xs, tuple)
    tile = (ROW_TILE, D_MODEL)
    pair = [pl.BlockSpec(tile, _prompt_only(n_p)), pl.BlockSpec(tile, _sample_only(n_p))]
    whole = [pl.BlockSpec(tile, lambda i: (i, 0))]
    if split_out:
        out_shape = [jax.ShapeDtypeStruct((rows_p, D_MODEL), F32), jax.ShapeDtypeStruct((rows_s, D_MODEL), F32)]
    else:
        out_shape = [jax.ShapeDtypeStruct((rows, D_MODEL), F32)]
    out = pl.pallas_call(
        functools.partial(_ffn_kernel, n_p=n_p, split_in=split_in, split_out=split_out),
        grid=(rows // ROW_TILE,),
        in_specs=(pair if split_in else whole) + [
            pl.BlockSpec((None, 1, D_MODEL), lambda i: (layer, 0, 0)),
            _resident((None, D_MODEL, 2 * FFN_HIDDEN), lambda i: (layer, 0, 0)),
            _resident((None, FFN_HIDDEN, D_MODEL), lambda i: (layer, 0, 0)),
        ],
        out_specs=pair if split_out else whole,
        out_shape=out_shape,
        compiler_params=_params(("arbitrary",)),
        name="ffn",
    )(*(xs if split_in else (xs,)), norm, w1, w2)
    return tuple(out) if split_out else out[0]


def _head_rms(x, gamma):
    m = x.shape[0]
    lane = lax.broadcasted_iota(jnp.int32, (m, LANES), 1)
    first = lane < HEAD_DIM
    outs = []
    for c in range(WIDTH_A // LANES):
        xc = x[:, c * LANES:(c + 1) * LANES]
        sq = xc * xc
        s_lo = jnp.sum(jnp.where(first, sq, 0.0), axis=-1, keepdims=True)
        s_hi = jnp.sum(jnp.where(first, 0.0, sq), axis=-1, keepdims=True)
        ms = jnp.where(first, s_lo, s_hi) * (1.0 / HEAD_DIM)
        outs.append((xc * lax.rsqrt(ms + RMS_EPS)) * gamma[:, c * LANES:(c + 1) * LANES])
    return jnp.concatenate(outs, axis=-1)


def _inproj_kernel(x_ref, g_ref, wa_ref, wm_ref, wf_ref, bf_ref, qn_ref, kn_ref, *rest, n_p, n_alias):
    (qb_ref, kb_ref, vb_ref, qs_ref, ks_ref, vs_ref, lfs_ref,
     kt_ref, vt_ref, lft_ref, mix_ref, gate_ref) = rest[n_alias:]
    i = pl.program_id(0)
    xn = _rms(x_ref[...], g_ref[...]).astype(BF16)
    qkv = _dot(xn, wa_ref[...])
    q = _head_rms(qkv[:, 0:WIDTH_A], qn_ref[...]) * (HEAD_DIM ** -0.5 * LOG2E)
    k = _head_rms(qkv[:, WIDTH_A:2 * WIDTH_A], kn_ref[...])
    v = qkv[:, 2 * WIDTH_A:3 * WIDTH_A]
    qb_ref[...] = q.astype(BF16)
    kb_ref[...] = k.astype(BF16)
    vb_ref[...] = v.astype(BF16)
    mix_ref[...] = _dot(xn, wm_ref[:, 0:W_MIX])
    gate_ref[...] = jax.nn.sigmoid(_dot(xn, wm_ref[:, W_MIX:W_MIX + W_G])).astype(BF16)
    f = _dot(xn, wf_ref[...]) + bf_ref[...]
    lf = jnp.minimum(f, 0.0) - jnp.log1p(jnp.exp(-jnp.abs(f)))

    @pl.when(i < n_p)
    def _():
        kt_ref[...] = k.T
        vt_ref[...] = v.T
        lft_ref[...] = lf.T[0:N_HEADS, :]

    @pl.when(i >= n_p)
    def _():
        qs_ref[...] = q
        ks_ref[...] = k
        vs_ref[...] = v
        lfs_ref[...] = lf


def _inproj(x, norm, w, bf, qn, kn, layer, depth, batch, seq, rows_s, prev):
    rows = x.shape[0]
    n_p = batch * seq // ROW_TILE
    per_seq = seq // ROW_TILE
    row_spec = lambda w_: pl.BlockSpec((ROW_TILE, w_), lambda i: (i, 0))
    smp_spec = lambda w_: pl.BlockSpec((ROW_TILE, w_), _sample_only(n_p))
    vec_spec = lambda w_: pl.BlockSpec((None, 1, w_), lambda i: (layer, 0, 0))

    def state_spec(r):
        def im(i):
            j = jnp.minimum(i, n_p - 1)
            return (j // per_seq, layer, 0, j % per_seq)
        return pl.BlockSpec((None, None, r, ROW_TILE), im)

    n_alias = 0 if prev is None else 3
    w_spec = lambda w_: _resident((None, D_MODEL, w_), lambda i: (layer, 0, 0))
    in_specs = [row_spec(D_MODEL), vec_spec(D_MODEL), w_spec(3 * WIDTH_A), w_spec(W_MIX + W_G), w_spec(LANES),
                vec_spec(LANES), vec_spec(WIDTH_A), vec_spec(WIDTH_A)]
    n_in = len(in_specs)
    in_specs += [pl.BlockSpec(memory_space=pl.ANY)] * n_alias
    sds = jax.ShapeDtypeStruct
    return pl.pallas_call(
        functools.partial(_inproj_kernel, n_p=n_p, n_alias=n_alias),
        grid=(rows // ROW_TILE,),
        in_specs=in_specs,
        out_specs=[
            row_spec(WIDTH_A), row_spec(WIDTH_A), row_spec(WIDTH_A),
            smp_spec(WIDTH_A), smp_spec(WIDTH_A), smp_spec(WIDTH_A), smp_spec(LANES),
            state_spec(WIDTH_A), state_spec(WIDTH_A), state_spec(N_HEADS),
            row_spec(W_MIX), row_spec(W_G),
        ],
        out_shape=[
            sds((rows, WIDTH_A), BF16), sds((rows, WIDTH_A), BF16), sds((rows, WIDTH_A), BF16),
            sds((rows_s, WIDTH_A), F32), sds((rows_s, WIDTH_A), F32), sds((rows_s, WIDTH_A), F32),
            sds((rows_s, LANES), F32),
            sds((batch, depth, WIDTH_A, seq), F32), sds((batch, depth, WIDTH_A, seq), F32),
            sds((batch, depth, N_HEADS, seq), F32),
            sds((rows, W_MIX), F32), sds((rows, W_G), BF16),
        ],
        input_output_aliases={n_in + a: 7 + a for a in range(n_alias)},
        compiler_params=_params(("arbitrary",)),
        name="inproj",
    )(x, norm, *w, bf, qn, kn, *(prev or ()))


def _lane_prefix(x, n_valid):
    lane = lax.broadcasted_iota(jnp.int32, x.shape, 1)
    shift = 1
    while shift < n_valid:
        x = x + jnp.where(lane >= shift, pltpu.roll(x, shift, 1), 0.0)
        shift *= 2
    return x


def _cumsum_kernel(x_ref, o_ref):
    o_ref[...] = _lane_prefix(x_ref[...], x_ref.shape[1]) * LOG2E


def _cumsum(lft, layer, batch, seq):
    return pl.pallas_call(
        _cumsum_kernel,
        grid=(batch,),
        in_specs=[pl.BlockSpec((None, None, N_HEADS, seq), lambda b: (b, layer, 0, 0))],
        out_specs=pl.BlockSpec((N_HEADS, seq), lambda b: (0, b)),
        out_shape=jax.ShapeDtypeStruct((N_HEADS, batch * seq), F32),
        compiler_params=_params(("parallel",)),
        name="cumsum",
    )(lft)


def _pool_prefix_kernel(x_ref, o_ref):
    r = lax.broadcasted_iota(jnp.int32, (PAGE, PAGE), 0)
    c = lax.broadcasted_iota(jnp.int32, (PAGE, PAGE), 1)
    tri = jnp.where(r <= c, 1.0, 0.0).astype(F32)
    o_ref[...] = jnp.dot(x_ref[...], tri, precision=lax.Precision.HIGHEST, preferred_element_type=F32) * LOG2E


def _pool_prefix(pool):
    rows = pool.shape[0]
    return pl.pallas_call(
        _pool_prefix_kernel,
        grid=(rows // POOL_TILE,),
        in_specs=[pl.BlockSpec((POOL_TILE, PAGE), lambda i: (i, 0))],
        out_specs=pl.BlockSpec((POOL_TILE, PAGE), lambda i: (i, 0)),
        out_shape=jax.ShapeDtypeStruct(pool.shape, F32),
        compiler_params=_params(("parallel",)),
        name="pool_prefix",
    )(pool)


def _attn_p_kernel(q_ref, k_ref, v_ref, c_ref, ct_ref, o_ref, m_ref, acc_ref, cq_ref, vx_ref):
    t = ATT_TILE
    n_heads = 2 * ATT_COLS
    head0 = pl.program_id(1) * n_heads
    qi = pl.program_id(2)
    lane = lax.broadcasted_iota(jnp.int32, (t, LANES), 1)
    first = lane < HEAD_DIM
    cols = lambda c: slice(c * LANES, (c + 1) * LANES)

    @pl.when(qi == 0)
    def _():
        own = lax.broadcasted_iota(jnp.int32, (v_ref.shape[0], LANES), 1) < HEAD_DIM
        for c in range(ATT_COLS):
            v = v_ref[:, cols(c)]
            one = jnp.ones_like(v)
            vx_ref[2 * c] = jnp.where(own, v, one)
            vx_ref[2 * c + 1] = jnp.where(own, one, v)

    q_heads = []
    for c in range(ATT_COLS):
        q2 = q_ref[:, cols(c)]
        zero = jnp.zeros_like(q2)
        q_heads += [jnp.where(first, q2, zero), jnp.where(first, zero, q2)]
    crow = c_ref[...]
    hlane = lax.broadcasted_iota(jnp.int32, crow.shape, 1)
    for e in range(n_heads):
        cq = jnp.sum(jnp.where(hlane == head0 + e, crow, 0.0), axis=-1, keepdims=True)
        cq_ref[e] = jnp.broadcast_to(cq, (t, LANES))
    m_ref[...] = jnp.full(m_ref.shape, NEG, F32)
    acc_ref[...] = jnp.zeros(acc_ref.shape, F32)
    row = lax.broadcasted_iota(jnp.int32, (t, t), 0)
    col = lax.broadcasted_iota(jnp.int32, (t, t), 1)
    causal = col <= row
    wide = lambda x: jnp.concatenate([x] * (t // LANES), axis=1)

    def step(j, masked):
        start = pl.multiple_of(j * t, t)
        for e in range(n_heads):
            kc = k_ref[pl.ds(start, t), cols(e // 2)]
            ck = ct_ref[pl.ds(head0 + e, 1), pl.ds(start, t)]
            u = _dot_nt(q_heads[e], kc) - ck
            if masked:
                u = jnp.where(causal, u, NEG)
            cq = cq_ref[e]
            m_prev = m_ref[e]
            m_new = jnp.maximum(m_prev, jnp.max(u, axis=-1, keepdims=True) + cq)
            alpha = jnp.exp2(m_prev - m_new)
            p = jnp.exp2(u + wide(cq - m_new))
            acc_ref[e] = alpha * acc_ref[e] + _dot(p.astype(BF16), vx_ref[e, pl.ds(start, t), :])
            m_ref[e] = m_new

    def body(j, carry):
        step(j, False)
        return carry

    lax.fori_loop(0, qi, body, 0)
    step(qi, True)
    for c in range(ATT_COLS):
        a0 = acc_ref[2 * c]
        a1 = acc_ref[2 * c + 1]
        o_ref[:, cols(c)] = jnp.where(first, a0 / pltpu.roll(a0, HEAD_DIM, 1), a1 / pltpu.roll(a1, HEAD_DIM, 1))


def _attn_p(qb, kb, vb, c_rows, ct, batch, seq):
    t = ATT_TILE
    nq = seq // t
    w = ATT_COLS * LANES
    return pl.pallas_call(
        _attn_p_kernel,
        grid=(batch, WIDTH_A // w, nq),
        in_specs=[
            pl.BlockSpec((t, w), lambda b, h, i: (b * nq + i, h)),
            pl.BlockSpec((seq, w), lambda b, h, i: (b, h)),
            pl.BlockSpec((seq, w), lambda b, h, i: (b, h)),
            pl.BlockSpec((t, N_HEADS), lambda b, h, i: (b * nq + i, 0)),
            pl.BlockSpec((N_HEADS, seq), lambda b, h, i: (0, b)),
        ],
        out_specs=pl.BlockSpec((t, w), lambda b, h, i: (b * nq + i, h)),
        out_shape=jax.ShapeDtypeStruct((qb.shape[0], WIDTH_A), F32),
        scratch_shapes=[
            pltpu.VMEM((2 * ATT_COLS, t, LANES), F32),
            pltpu.VMEM((2 * ATT_COLS, t, LANES), F32),
            pltpu.VMEM((2 * ATT_COLS, t, LANES), F32),
            pltpu.VMEM((2 * ATT_COLS, seq, LANES), BF16),
        ],
        compiler_params=_params(("parallel", "parallel", "arbitrary")),
        name="attn_p",
    )(qb, kb, vb, c_rows, ct)


def _transpose_rows8(x):
    pad = jnp.concatenate([x, jnp.zeros((LANES - SUBLANES, LANES), F32)], axis=0)
    return pad.T[0:SUBLANES, :]


def _attn_s_kernel(pt_ref, q_ref, kn_ref, vn_ref, lfn_ref, ck_hbm, cv_hbm, cc_hbm, o_alias, o_ref,
                   kbuf, vbuf, cbuf, sem, *, n_pages, t_new, layer, n_seq):
    n = pl.program_id(0)
    slot = lax.rem(n, PAGE_SLOTS)
    ahead = jnp.minimum(n + PAGE_AHEAD, n_seq - 1)
    slot_ahead = lax.rem(n + PAGE_AHEAD, PAGE_SLOTS)

    def page_copies(seq, slot_, j):
        pg = pt_ref[seq, j]
        return (pltpu.make_async_copy(ck_hbm.at[pg, layer], kbuf.at[slot_, j], sem.at[0, slot_]),
                pltpu.make_async_copy(cv_hbm.at[pg, layer], vbuf.at[slot_, j], sem.at[1, slot_]),
                pltpu.make_async_copy(cc_hbm.at[pg, layer], cbuf.at[slot_, j], sem.at[2, slot_]))

    @pl.when(n == 0)
    def _():
        for s in range(PAGE_AHEAD):
            for j in range(n_pages):
                for c in page_copies(min(s, n_seq - 1), s, j):
                    c.start()

    for j in range(n_pages):
        for c in page_copies(n, slot, j):
            c.wait()
    k_pages, v_pages, c_pages = [], [], []
    for j in range(n_pages):
        k_pages.append(kbuf[slot, j].reshape(WIDTH_A, PAGE).astype(BF16))
        v_pages.append(vbuf[slot, j].reshape(WIDTH_A, PAGE).astype(BF16))
        c_pages.append(cbuf[slot, j])
        for c in page_copies(ahead, slot_ahead, j):
            c.start()

    rows = N_HEADS * t_new
    rhead = lax.broadcasted_iota(jnp.int32, (rows, WIDTH_A), 0) // t_new
    chead = lax.broadcasted_iota(jnp.int32, (rows, WIDTH_A), 1) // HEAD_DIM
    diag = rhead == chead
    q = q_ref[...]
    qbd = jnp.where(diag, jnp.concatenate([q] * N_HEADS, axis=0), 0.0).astype(BF16)

    def rep(ct):
        return jnp.broadcast_to(ct[:, None, :], (N_HEADS, t_new, LANES)).reshape(rows, LANES)

    carry = jnp.zeros((N_HEADS, 1), F32)
    cts = []
    for within in c_pages:
        cts.append(within + carry)
        carry = carry + within[:, PAGE - 1:PAGE]
    lft_new = _transpose_rows8(lfn_ref[...])
    ct_new = _lane_prefix(lft_new, t_new) * LOG2E + carry
    lane = lax.broadcasted_iota(jnp.int32, (rows, LANES), 1)
    tq = lax.broadcasted_iota(jnp.int32, (rows, LANES), 0) % t_new
    ct_new_rep = rep(ct_new)
    cq = jnp.sum(jnp.where(lane == tq, ct_new_rep, 0.0), axis=-1, keepdims=True)

    n_past = n_pages * PAGE
    k_past = jnp.concatenate(k_pages, axis=1)
    v_past = jnp.concatenate(v_pages, axis=1)
    ct_past = jnp.concatenate(cts, axis=1)
    ct_past_rep = jnp.broadcast_to(ct_past[:, None, :], (N_HEADS, t_new, n_past)).reshape(rows, n_past)
    u_past = _dot(qbd, k_past) - ct_past_rep
    zeros_pad = jnp.zeros((PAGE - t_new, WIDTH_A), F32)
    k_new = jnp.concatenate([kn_ref[...], zeros_pad], axis=0).astype(BF16)
    v_new = jnp.concatenate([vn_ref[...], zeros_pad], axis=0).astype(BF16)
    u_new = jnp.where(lane <= tq, _dot_nt(qbd, k_new) - ct_new_rep, NEG)

    m = jnp.maximum(jnp.max(u_past, axis=-1, keepdims=True), jnp.max(u_new, axis=-1, keepdims=True)) + cq
    p_past = jnp.exp2((u_past + cq) - m)
    p_new = jnp.exp2((u_new + cq) - m)
    l = jnp.sum(p_past, axis=-1, keepdims=True) + jnp.sum(p_new, axis=-1, keepdims=True)
    acc = _dot_nt(p_past.astype(BF16), v_past) + _dot(p_new.astype(BF16), v_new)
    o = jnp.where(diag, acc / l, 0.0)
    out = o[0:t_new, :]
    for h in range(1, N_HEADS):
        out = out + o[h * t_new:(h + 1) * t_new, :]
    o_ref[...] = out

    @pl.when(n == n_seq - 1)
    def _():
        for s in range(1, PAGE_AHEAD + 1):
            for j in range(n_pages):
                for c in page_copies(n, lax.rem(n + s, PAGE_SLOTS), j):
                    c.wait()


def _attn_s(page_table, q, k, v, lf, cache_k, cache_v, cache_c, oa, layer, row0, t_new):
    n_seq, n_pages = page_table.shape
    blk0 = row0 // t_new
    new_spec = lambda w_: pl.BlockSpec((t_new, w_), lambda n, pt: (n, 0))
    in_specs = [new_spec(WIDTH_A), new_spec(WIDTH_A), new_spec(WIDTH_A), new_spec(LANES)]
    in_specs += [pl.BlockSpec(memory_space=pl.ANY)] * 4
    grid_spec = pltpu.PrefetchScalarGridSpec(
        num_scalar_prefetch=1,
        grid=(n_seq,),
        in_specs=in_specs,
        out_specs=pl.BlockSpec((t_new, WIDTH_A), lambda n, pt: (blk0 + n, 0)),
        scratch_shapes=[
            pltpu.VMEM((PAGE_SLOTS, n_pages, N_HEADS, HEAD_DIM, PAGE), F32),
            pltpu.VMEM((PAGE_SLOTS, n_pages, N_HEADS, HEAD_DIM, PAGE), F32),
            pltpu.VMEM((PAGE_SLOTS, n_pages, N_HEADS, PAGE), F32),
            pltpu.SemaphoreType.DMA((3, PAGE_SLOTS)),
        ],
    )
    return pl.pallas_call(
        functools.partial(_attn_s_kernel, n_pages=n_pages, t_new=t_new, layer=layer, n_seq=n_seq),
        grid_spec=grid_spec,
        out_shape=jax.ShapeDtypeStruct(oa.shape, oa.dtype),
        input_output_aliases={len(in_specs): 0},
        compiler_params=_params(("arbitrary",)),
        name="attn_s",
    )(page_table, q, k, v, lf, cache_k, cache_v, cache_c, oa)


def _lru_coeffs(xc, wa_ref, ba_ref, wx_ref, bx_ref, lam_ref):
    xb = xc.astype(BF16)
    r = jax.nn.sigmoid(_dot(xb, wa_ref[...]) + ba_ref[...])
    ig = jax.nn.sigmoid(_dot(xb, wx_ref[...]) + bx_ref[...])
    log_a = (-LRU_C * r) * _softplus(-lam_ref[...])
    a = jnp.exp(log_a)
    b = jnp.sqrt(1.0 - a * a) * (ig * xc)
    return a, b


def _mix_p_kernel(z_ref, cw_ref, cb_ref, wa_ref, ba_ref, wx_ref, bx_ref, lam_ref, sw_ref,
                  ob_ref, oc_ref, h_ref, lc_ref, scc_ref,
                  ubuf, pbuf, a_s, b_s, hcar):
    tt = MIX_TILE
    hist = SUBLANES
    i = pl.program_id(1)

    @pl.when(i == 0)
    def _():
        ubuf[0:hist, :] = jnp.zeros((hist, WIDTH_B), F32)
        pbuf[0:hist, :] = jnp.zeros((hist, WIDTH_C), F32)
        hcar[...] = jnp.zeros(hcar.shape, F32)

    u = z_ref[:, 0:WIDTH_B]
    ubuf[hist:hist + tt, :] = u
    xc = cw_ref[0:1, :] * ubuf[hist - 3:hist - 3 + tt, :]
    xc = xc + cw_ref[1:2, :] * ubuf[hist - 2:hist - 2 + tt, :]
    xc = xc + cw_ref[2:3, :] * ubuf[hist - 1:hist - 1 + tt, :]
    xc = xc + cw_ref[3:4, :] * u
    xc = xc + cb_ref[...]
    ubuf[0:hist, :] = ubuf[tt:tt + hist, :]
    lc_ref[...] = z_ref[tt - (CONV_B - 1):tt, 0:WIDTH_B]

    a, b = _lru_coeffs(xc, wa_ref, ba_ref, wx_ref, bx_ref, lam_ref)
    a_s[...] = a
    b_s[...] = b
    srow = lax.broadcasted_iota(jnp.int32, (SUBLANES, WIDTH_B), 0)

    def chunk(c, h):
        r0 = pl.multiple_of(c * SUBLANES, SUBLANES)
        aa = a_s[pl.ds(r0, SUBLANES), :]
        bb = b_s[pl.ds(r0, SUBLANES), :]
        for s in (1, 2, 4):
            keep = srow >= s
            a_sh = jnp.where(keep, pltpu.roll(aa, s, 0), 1.0)
            b_sh = jnp.where(keep, pltpu.roll(bb, s, 0), 0.0)
            bb = aa * b_sh + bb
            aa = aa * a_sh
        hc = aa * h + bb
        b_s[pl.ds(r0, SUBLANES), :] = hc
        return hc[SUBLANES - 1:SUBLANES, :]

    h_last = lax.fori_loop(0, tt // SUBLANES, chunk, hcar[...], unroll=4)
    hcar[...] = h_last
    h_ref[...] = h_last
    ob_ref[...] = b_s[...] * jax.nn.gelu(z_ref[:, WIDTH_B:2 * WIDTH_B])

    o3 = 2 * WIDTH_B
    pc = z_ref[:, o3 + WIDTH_C:o3 + 2 * WIDTH_C] * z_ref[:, o3 + 2 * WIDTH_C:o3 + 3 * WIDTH_C]
    pbuf[hist:hist + tt, :] = pc
    uc = sw_ref[0:1, :] * pbuf[hist - 2:hist - 2 + tt, :]
    uc = uc + sw_ref[1:2, :] * pbuf[hist - 1:hist - 1 + tt, :]
    uc = uc + sw_ref[2:3, :] * pc
    pbuf[0:hist, :] = pbuf[tt:tt + hist, :]
    scc_ref[...] = pbuf[hist + tt - (CONV_C - 1):hist + tt, :]
    oc_ref[...] = z_ref[:, o3:o3 + WIDTH_C] * uc


def _mix_weight_specs(layer):
    def im(*_):
        return (layer, 0, 0)

    return [
        pl.BlockSpec((None, CONV_B, WIDTH_B), im),
        pl.BlockSpec((None, 1, WIDTH_B), im),
        pl.BlockSpec((None, WIDTH_B, WIDTH_B), im),
        pl.BlockSpec((None, 1, WIDTH_B), im),
        pl.BlockSpec((None, WIDTH_B, WIDTH_B), im),
        pl.BlockSpec((None, 1, WIDTH_B), im),
        pl.BlockSpec((None, 1, WIDTH_B), im),
        pl.BlockSpec((None, CONV_C, WIDTH_C), im),
    ]


def _mix_p(zmix, weights, layer, batch, seq):
    tt = MIX_TILE
    nt = seq // tt
    rows = zmix.shape[0]
    return pl.pallas_call(
        _mix_p_kernel,
        grid=(batch, nt),
        in_specs=[pl.BlockSpec((tt, W_MIX), lambda b, i: (b * nt + i, 0))] + _mix_weight_specs(layer),
        out_specs=[
            pl.BlockSpec((tt, WIDTH_B), lambda b, i: (b * nt + i, 0)),
            pl.BlockSpec((tt, WIDTH_C), lambda b, i: (b * nt + i, 0)),
            pl.BlockSpec((None, 1, WIDTH_B), lambda b, i: (b, 0, 0)),
            pl.BlockSpec((None, CONV_B - 1, WIDTH_B), lambda b, i: (b, 0, 0)),
            pl.BlockSpec((None, CONV_C - 1, WIDTH_C), lambda b, i: (b, 0, 0)),
        ],
        out_shape=[
            jax.ShapeDtypeStruct((rows, WIDTH_B), F32),
            jax.ShapeDtypeStruct((rows, WIDTH_C), F32),
            jax.ShapeDtypeStruct((batch, 1, WIDTH_B), F32),
            jax.ShapeDtypeStruct((batch, CONV_B - 1, WIDTH_B), F32),
            jax.ShapeDtypeStruct((batch, CONV_C - 1, WIDTH_C), F32),
        ],
        scratch_shapes=[
            pltpu.VMEM((tt + SUBLANES, WIDTH_B), F32),
            pltpu.VMEM((tt + SUBLANES, WIDTH_C), F32),
            pltpu.VMEM((tt, WIDTH_B), F32),
            pltpu.VMEM((tt, WIDTH_B), F32),
            pltpu.VMEM((1, WIDTH_B), F32),
        ],
        compiler_params=_params(("arbitrary", "arbitrary")),
        name="mix_p",
    )(zmix, *weights)


def _mix_s_kernel(z_ref, h0_ref, lch_ref, sch_ref, cw_ref, cb_ref, wa_ref, ba_ref, wx_ref, bx_ref,
                  lam_ref, sw_ref, ob_alias, oc_alias, ob_ref, oc_ref, hs_ref, pc_ref, *, t_new):
    tt = z_ref.shape[0]
    o3 = 2 * WIDTH_B
    tpos = lax.broadcasted_iota(jnp.int32, (tt, WIDTH_B), 0) % t_new

    def back(x, hist, s):
        return jnp.where(tpos >= s, pltpu.roll(x, s, 0), pltpu.roll(hist, tt - (t_new - s), 0))

    u = z_ref[:, 0:WIDTH_B]
    lch = lch_ref[...]
    xc = cw_ref[0:1, :] * back(u, lch, 3)
    xc = xc + cw_ref[1:2, :] * back(u, lch, 2)
    xc = xc + cw_ref[2:3, :] * back(u, lch, 1)
    xc = xc + cw_ref[3:4, :] * u
    xc = xc + cb_ref[...]
    a, b = _lru_coeffs(xc, wa_ref, ba_ref, wx_ref, bx_ref, lam_ref)
    b = b + a * h0_ref[...]
    for s in (1, 2, 4):
        keep = tpos >= s
        a_sh = jnp.where(keep, pltpu.roll(a, s, 0), 1.0)
        b_sh = jnp.where(keep, pltpu.roll(b, s, 0), 0.0)
        b = a * b_sh + b
        a = a * a_sh
    hs_ref[...] = b
    ob_ref[...] = b * jax.nn.gelu(z_ref[:, WIDTH_B:2 * WIDTH_B])

    pc = z_ref[:, o3 + WIDTH_C:o3 + 2 * WIDTH_C] * z_ref[:, o3 + 2 * WIDTH_C:o3 + 3 * WIDTH_C]
    sch = sch_ref[...]
    uc = sw_ref[0:1, :] * back(pc, sch, 2)
    uc = uc + sw_ref[1:2, :] * back(pc, sch, 1)
    uc = uc + sw_ref[2:3, :] * pc
    pc_ref[...] = pc
    oc_ref[...] = z_ref[:, o3:o3 + WIDTH_C] * uc


def _mix_s(zmix, h0e, lch, sch, weights, ob, oc, layer, row0, n_seq, t_new):
    ns = SEQ_PER_MIX_STEP
    tt = ns * t_new
    blk0 = row0 // tt
    rows = n_seq * t_new
    row_spec = lambda w_: pl.BlockSpec((tt, w_), lambda i: (i, 0))
    all_spec = lambda w_: pl.BlockSpec((tt, w_), lambda i: (blk0 + i, 0))
    in_specs = [all_spec(W_MIX), row_spec(WIDTH_B), row_spec(WIDTH_B), row_spec(WIDTH_C)]
    in_specs += _mix_weight_specs(layer)
    in_specs += [pl.BlockSpec(memory_space=pl.ANY)] * 2
    return pl.pallas_call(
        functools.partial(_mix_s_kernel, t_new=t_new),
        grid=(n_seq // ns,),
        in_specs=in_specs,
        out_specs=[all_spec(WIDTH_B), all_spec(WIDTH_C), row_spec(WIDTH_B), row_spec(WIDTH_C)],
        out_shape=[
            jax.ShapeDtypeStruct(ob.shape, ob.dtype),
            jax.ShapeDtypeStruct(oc.shape, oc.dtype),
            jax.ShapeDtypeStruct((rows, WIDTH_B), F32),
            jax.ShapeDtypeStruct((rows, WIDTH_C), F32),
        ],
        input_output_aliases={len(in_specs) - 2: 0, len(in_specs) - 1: 1},
        compiler_params=_params(("parallel",)),
        name="mix_s",
    )(zmix, h0e, lch, sch, *weights, ob, oc)


def _merge_kernel(x_ref, oa_ref, ob_ref, oc_ref, g_ref, wb_ref, wo_ref, o_ref):
    merged = None
    lo = 0
    for b, (o_b, width) in enumerate(((oa_ref, WIDTH_A), (ob_ref, WIDTH_B), (oc_ref, WIDTH_C))):
        proj = _dot(o_b[...].astype(BF16), wb_ref[lo:lo + width, :])
        term = g_ref[:, b * D_MODEL:(b + 1) * D_MODEL].astype(F32) * proj
        merged = term if merged is None else merged + term
        lo += width
    o_ref[...] = x_ref[...] + _dot(merged.astype(BF16), wo_ref[...])


def _merge(x, oa, ob, oc, g, wb, wo, layer):
    rows = x.shape[0]
    row_spec = lambda w_: pl.BlockSpec((ROW_TILE, w_), lambda i: (i, 0))
    return pl.pallas_call(
        _merge_kernel,
        grid=(rows // ROW_TILE,),
        in_specs=[
            row_spec(D_MODEL), row_spec(WIDTH_A), row_spec(WIDTH_B), row_spec(WIDTH_C), row_spec(W_G),
            _resident((None, WIDTH_A + WIDTH_B + WIDTH_C, D_MODEL), lambda i: (layer, 0, 0)),
            _resident((None, D_MODEL, D_MODEL), lambda i: (layer, 0, 0)),
        ],
        out_specs=row_spec(D_MODEL),
        out_shape=jax.ShapeDtypeStruct((rows, D_MODEL), F32),
        compiler_params=_params(("parallel",)),
        name="merge",
    )(x, oa, ob, oc, g, wb, wo)


def kernel(x_prompt, x_sample, cache_k, cache_v, cache_logf, state_lru_h, state_lru_conv, state_sc_conv,
           page_table, ffn1_norm, ffn1_w_in, ffn1_w_out, mix_norm, w_in, b_forget, q_norm, k_norm,
           lru_conv_w, lru_conv_b, lru_wa, lru_ba, lru_wx, lru_bx, lru_lambda, sc_conv_w, w_branch, w_out,
           ffn2_norm, ffn2_w_in, ffn2_w_out):
    batch, seq, _ = x_prompt.shape
    n_seq, t_new, _ = x_sample.shape
    depth = w_in.shape[0]
    rows_p = batch * seq
    rows_s = n_seq * t_new

    vec = lambda a: a[:, None, :]
    w1a, w2a = ffn1_w_in.astype(BF16), ffn1_w_out.astype(BF16)
    w1b, w2b = ffn2_w_in.astype(BF16), ffn2_w_out.astype(BF16)
    w_in_p = (w_in[:, :, :OFF_F].astype(BF16), w_in[:, :, OFF_MIX:].astype(BF16),
              jnp.pad(w_in[:, :, OFF_F:OFF_MIX], ((0, 0), (0, 0), (0, LANES - N_HEADS))).astype(BF16))
    bf_p = jnp.pad(b_forget, ((0, 0), (0, LANES - N_HEADS)))[:, None, :]
    qn_p = vec(jnp.tile(q_norm, (1, N_HEADS)))
    kn_p = vec(jnp.tile(k_norm, (1, N_HEADS)))
    eye = jnp.eye(N_BLOCKS_B, dtype=lru_wa.dtype)
    dense = lambda w: jnp.einsum('lgij,gh->lgihj', w, eye).reshape(depth, WIDTH_B, WIDTH_B).astype(BF16)
    mix_weights = (lru_conv_w, vec(lru_conv_b), dense(lru_wa), vec(lru_ba), dense(lru_wx), vec(lru_bx),
                   vec(lru_lambda), sc_conv_w)
    wb_b, wo_b = w_branch.astype(BF16), w_out.astype(BF16)
    ck4 = jnp.transpose(cache_k, (0, 1, 3, 4, 2))
    cv4 = jnp.transpose(cache_v, (0, 1, 3, 4, 2))
    clft = jnp.swapaxes(cache_logf, 2, 3)
    cpool = _pool_prefix(clft.reshape(-1, PAGE)).reshape(clft.shape)

    def seq_rows(state, at_end):
        r = state.shape[1]
        pad = (t_new - r, 0) if at_end else (0, t_new - r)
        return jnp.pad(state, ((0, 0), pad, (0, 0))).reshape(rows_s, state.shape[-1])

    per_seq = lambda a: a.reshape(n_seq, t_new, a.shape[-1])
    x = (x_prompt.reshape(rows_p, D_MODEL), x_sample.reshape(rows_s, D_MODEL))
    prev = None
    outs_p = [[] for _ in range(3)]
    outs_s = [[] for _ in range(6)]
    for l in range(depth):
        x = _ffn(x, vec(ffn1_norm), w1a, w2a, l, rows_p, rows_s, False)
        (qb, kb, vb, qs, ks, vs, lfs, kt, vt, lft, zmix, gates) = _inproj(
            x, vec(mix_norm), w_in_p, bf_p, qn_p, kn_p, l, depth, batch, seq, rows_s, prev)
        prev = (kt, vt, lft)
        ct = _cumsum(lft, l, batch, seq)
        oa = _attn_p(qb, kb, vb, ct.T, ct, batch, seq)
        oa = _attn_s(page_table, qs, ks, vs, lfs, ck4, cv4, cpool, oa, l, rows_p, t_new)
        ob, oc, h_p, lc_p, sc_p = _mix_p(zmix, mix_weights, l, batch, seq)
        ob, oc, hs_s, pc_s = _mix_s(zmix, seq_rows(state_lru_h[:, l][:, None, :], False),
                                    seq_rows(state_lru_conv[:, l], True),
                                    seq_rows(state_sc_conv[:, l], True), mix_weights, ob, oc,
                                    l, rows_p, n_seq, t_new)
        x = _merge(x, oa, ob, oc, gates, wb_b, wo_b, l)
        x = _ffn(x, vec(ffn2_norm), w1b, w2b, l, rows_p, rows_s, l == depth - 1)
        for dst, val in zip(outs_p, (h_p[:, 0], lc_p, sc_p)):
            dst.append(val)
        for dst, val in zip(outs_s, (ks.reshape(n_seq, t_new, N_HEADS, HEAD_DIM),
                                     vs.reshape(n_seq, t_new, N_HEADS, HEAD_DIM),
                                     per_seq(lfs[:, :N_HEADS]),
                                     per_seq(hs_s)[:, -1],
                                     per_seq(zmix[rows_p:, :WIDTH_B])[:, t_new - (CONV_B - 1):],
                                     per_seq(pc_s)[:, t_new - (CONV_C - 1):])):
            dst.append(val)
    stack = lambda xs: jnp.stack(xs, axis=1)
    kt, vt, lft = prev
    heads = lambda a: jnp.transpose(a.reshape(batch, depth, N_HEADS, HEAD_DIM, seq), (0, 1, 4, 2, 3))
    y_p, y_s = x
    return ((y_p.reshape(batch, seq, D_MODEL), y_s.reshape(n_seq, t_new, D_MODEL),
             heads(kt), heads(vt), jnp.swapaxes(lft, 2, 3))
            + tuple(stack(o) for o in outs_p) + tuple(stack(o) for o in outs_s))
```

```python
import functools
import math

import jax
import jax.numpy as jnp
from jax import lax
from jax.experimental import pallas as pl
from jax.experimental.pallas import tpu as pltpu

F32 = jnp.float32
BF16 = jnp.bfloat16

D_MODEL = 1024
N_HEADS = 8
HEAD_DIM = 64
WIDTH_A = N_HEADS * HEAD_DIM
WIDTH_B = 512
N_BLOCKS_B = 8
CONV_B = 4
LRU_C = 8.0
WIDTH_C = 512
CONV_C = 3
N_BRANCH = 3
FFN_HIDDEN = 2816
RMS_EPS = 1e-6
PAGE = 128

LANES = 128
SUBLANES = 8
VMEM_LIMIT = 56 * 1024 * 1024
LOG2E = math.log2(math.e)

OFF_F = 3 * WIDTH_A
OFF_MIX = OFF_F + N_HEADS
W_MIX = 2 * WIDTH_B + 3 * WIDTH_C
OFF_G = OFF_MIX + W_MIX
W_G = N_BRANCH * D_MODEL

NEG = -1e30
ROW_TILE = 512
MXU_TILE = 256
FFN_CHUNKS = ((0, 5 * MXU_TILE), (5 * MXU_TILE, FFN_HIDDEN))
ATT_TILE = 512
ATT_COLS = 4
ATT_KTILE = 256
MIX_TILE = 512
SEQ_PER_MIX_STEP = 64
POOL_TILE = 2560
PAGE_AHEAD = 2
PAGE_SLOTS = PAGE_AHEAD + 1


def _params(sem):
    return pltpu.CompilerParams(dimension_semantics=sem, vmem_limit_bytes=VMEM_LIMIT)


def _resident(shape, index_map):
    return pl.BlockSpec(shape, index_map, pipeline_mode=pl.Buffered(1))


def _rms(x, g):
    return (x * lax.rsqrt(jnp.mean(x * x, axis=-1, keepdims=True) + RMS_EPS)) * g


def _softplus(x):
    return jnp.maximum(x, 0.0) + jnp.log1p(jnp.exp(-jnp.abs(x)))


def _dot(a, b):
    return jnp.dot(a, b, preferred_element_type=F32)


def _dot_nt(a, b):
    return lax.dot_general(a, b, (((1,), (1,)), ((), ())), preferred_element_type=F32)


def _prompt_only(n_p):
    return lambda i: (jnp.minimum(i, n_p - 1), 0)


def _sample_only(n_p):
    return lambda i: (jnp.maximum(i - n_p, 0), 0)


def _ffn_kernel(*refs, n_p, split_in, split_out):
    i = pl.program_id(0)
    n_x = 2 if split_in else 1
    g_ref, w1_ref, w2_ref = refs[n_x:n_x + 3]
    outs = refs[n_x + 3:]
    if split_in:
        x = jnp.where(i < n_p, refs[0][...], refs[1][...])
    else:
        x = refs[0][...]
    xn = _rms(x, g_ref[...]).astype(BF16)
    acc = jnp.zeros(x.shape, F32)
    for lo, hi in FFN_CHUNKS:
        gate = _dot(xn, w1_ref[:, lo:hi])
        up = _dot(xn, w1_ref[:, FFN_HIDDEN + lo:FFN_HIDDEN + hi])
        act = (gate * jax.nn.sigmoid(gate)) * up
        acc = acc + _dot(act.astype(BF16), w2_ref[lo:hi, :])
    y = x + 0.5 * acc
    if split_out:
        @pl.when(i < n_p)
        def _():
            outs[0][...] = y

        @pl.when(i >= n_p)
        def _():
            outs[1][...] = y
    else:
        outs[0][...] = y


def _ffn(xs, norm, w1, w2, layer, rows_p, rows_s, split_out):
    n_p = rows_p // ROW_TILE
    rows = rows_p + rows_s
    split_in = isinstance(xs, tuple)
    tile = (ROW_TILE, D_MODEL)
    pair = [pl.BlockSpec(tile, _prompt_only(n_p)), pl.BlockSpec(tile, _sample_only(n_p))]
    whole = [pl.BlockSpec(tile, lambda i: (i, 0))]
    if split_out:
        out_shape = [jax.ShapeDtypeStruct((rows_p, D_MODEL), F32), jax.ShapeDtypeStruct((rows_s, D_MODEL), F32)]
    else:
        out_shape = [jax.ShapeDtypeStruct((rows, D_MODEL), F32)]
    out = pl.pallas_call(
        functools.partial(_ffn_kernel, n_p=n_p, split_in=split_in, split_out=split_out),
        grid=(rows // ROW_TILE,),
        in_specs=(pair if split_in else whole) + [
            pl.BlockSpec((None, 1, D_MODEL), lambda i: (layer, 0, 0)),
            _resident((None, D_MODEL, 2 * FFN_HIDDEN), lambda i: (layer, 0, 0)),
            _resident((None, FFN_HIDDEN, D_MODEL), lambda i: (layer, 0, 0)),
        ],
        out_specs=pair if split_out else whole,
        out_shape=out_shape,
        compiler_params=_params(("arbitrary",)),
        name="ffn",
    )(*(xs if split_in else (xs,)), norm, w1, w2)
    return tuple(out) if split_out else out[0]


def _head_rms(x, gamma):
    m = x.shape[0]
    lane = lax.broadcasted_iota(jnp.int32, (m, LANES), 1)
    first = lane < HEAD_DIM
    outs = []
    for c in range(WIDTH_A // LANES):
        xc = x[:, c * LANES:(c + 1) * LANES]
        sq = xc * xc
        s_lo = jnp.sum(jnp.where(first, sq, 0.0), axis=-1, keepdims=True)
        s_hi = jnp.sum(jnp.where(first, 0.0, sq), axis=-1, keepdims=True)
        ms = jnp.where(first, s_lo, s_hi) * (1.0 / HEAD_DIM)
        outs.append((xc * lax.rsqrt(ms + RMS_EPS)) * gamma[:, c * LANES:(c + 1) * LANES])
    return jnp.concatenate(outs, axis=-1)


def _inproj_kernel(x_ref, g_ref, wa_ref, wm_ref, wf_ref, bf_ref, qn_ref, kn_ref, *rest, n_s, n_alias):
    (qb_ref, kb_ref, vb_ref, qs_ref, ks_ref, vs_ref, lfs_ref,
     kt_ref, vt_ref, lft_ref, mix_ref, gate_ref) = rest[n_alias:]
    i = pl.program_id(0)
    xn = _rms(x_ref[...], g_ref[...]).astype(BF16)
    qkv = _dot(xn, wa_ref[...])
    q = _head_rms(qkv[:, 0:WIDTH_A], qn_ref[...]) * (HEAD_DIM ** -0.5 * LOG2E)
    k = _head_rms(qkv[:, WIDTH_A:2 * WIDTH_A], kn_ref[...])
    v = qkv[:, 2 * WIDTH_A:3 * WIDTH_A]
    qb_ref[...] = q.astype(BF16)
    kb_ref[...] = k.astype(BF16)
    vb_ref[...] = v.astype(BF16)
    mix_ref[...] = _dot(xn, wm_ref[:, 0:W_MIX])
    gate_ref[...] = jax.nn.sigmoid(_dot(xn, wm_ref[:, W_MIX:W_MIX + W_G])).astype(BF16)
    f = _dot(xn, wf_ref[...]) + bf_ref[...]
    lf = jnp.minimum(f, 0.0) - jnp.log1p(jnp.exp(-jnp.abs(f)))

    kt_ref[...] = k.T
    vt_ref[...] = v.T
    lft_ref[...] = lf.T[0:N_HEADS, :]

    @pl.when(i < n_s)
    def _():
        qs_ref[...] = q
        ks_ref[...] = k
        vs_ref[...] = v
        lfs_ref[...] = lf


def _inproj(x, norm, w, bf, qn, kn, layer, depth, batch, seq, rows_s, prev):
    rows = x.shape[0]
    n_p = batch * seq // ROW_TILE
    n_s = rows_s // ROW_TILE
    per_seq = seq // ROW_TILE
    tile = lambda i: jnp.where(i < n_s, n_p + i, i - n_s)
    row_spec = lambda w_: pl.BlockSpec((ROW_TILE, w_), lambda i: (tile(i), 0))
    smp_spec = lambda w_: pl.BlockSpec((ROW_TILE, w_), lambda i: (jnp.minimum(i, n_s - 1), 0))
    vec_spec = lambda w_: pl.BlockSpec((None, 1, w_), lambda i: (layer, 0, 0))

    def state_spec(r):
        def im(i):
            j = jnp.maximum(i - n_s, 0)
            return (j // per_seq, layer, 0, j % per_seq)
        return pl.BlockSpec((None, None, r, ROW_TILE), im)

    n_alias = 0 if prev is None else 3
    w_spec = lambda w_: _resident((None, D_MODEL, w_), lambda i: (layer, 0, 0))
    in_specs = [row_spec(D_MODEL), vec_spec(D_MODEL), w_spec(3 * WIDTH_A), w_spec(W_MIX + W_G), w_spec(LANES),
                vec_spec(LANES), vec_spec(WIDTH_A), vec_spec(WIDTH_A)]
    n_in = len(in_specs)
    in_specs += [pl.BlockSpec(memory_space=pl.ANY)] * n_alias
    sds = jax.ShapeDtypeStruct
    return pl.pallas_call(
        functools.partial(_inproj_kernel, n_s=n_s, n_alias=n_alias),
        grid=(rows // ROW_TILE,),
        in_specs=in_specs,
        out_specs=[
            row_spec(WIDTH_A), row_spec(WIDTH_A), row_spec(WIDTH_A),
            smp_spec(WIDTH_A), smp_spec(WIDTH_A), smp_spec(WIDTH_A), smp_spec(LANES),
            state_spec(WIDTH_A), state_spec(WIDTH_A), state_spec(N_HEADS),
            row_spec(W_MIX), row_spec(W_G),
        ],
        out_shape=[
            sds((rows, WIDTH_A), BF16), sds((rows, WIDTH_A), BF16), sds((rows, WIDTH_A), BF16),
            sds((rows_s, WIDTH_A), F32), sds((rows_s, WIDTH_A), F32), sds((rows_s, WIDTH_A), F32),
            sds((rows_s, LANES), F32),
            sds((batch, depth, WIDTH_A, seq), F32), sds((batch, depth, WIDTH_A, seq), F32),
            sds((batch, depth, N_HEADS, seq), F32),
            sds((rows, W_MIX), F32), sds((rows, W_G), BF16),
        ],
        input_output_aliases={n_in + a: 7 + a for a in range(n_alias)},
        compiler_params=_params(("arbitrary",)),
        name="inproj",
    )(x, norm, *w, bf, qn, kn, *(prev or ()))


def _lane_prefix(x, n_valid):
    lane = lax.broadcasted_iota(jnp.int32, x.shape, 1)
    shift = 1
    while shift < n_valid:
        x = x + jnp.where(lane >= shift, pltpu.roll(x, shift, 1), 0.0)
        shift *= 2
    return x


def _cumsum_kernel(x_ref, o_ref):
    o_ref[...] = _lane_prefix(x_ref[...], x_ref.shape[1]) * LOG2E


def _cumsum(lft, layer, batch, seq):
    return pl.pallas_call(
        _cumsum_kernel,
        grid=(batch,),
        in_specs=[pl.BlockSpec((None, None, N_HEADS, seq), lambda b: (b, layer, 0, 0))],
        out_specs=pl.BlockSpec((N_HEADS, seq), lambda b: (0, b)),
        out_shape=jax.ShapeDtypeStruct((N_HEADS, batch * seq), F32),
        compiler_params=_params(("parallel",)),
        name="cumsum",
    )(lft)


def _pool_prefix_kernel(x_ref, o_ref):
    r = lax.broadcasted_iota(jnp.int32, (PAGE, PAGE), 0)
    c = lax.broadcasted_iota(jnp.int32, (PAGE, PAGE), 1)
    tri = jnp.where(r <= c, 1.0, 0.0).astype(F32)
    o_ref[...] = jnp.dot(x_ref[...], tri, precision=lax.Precision.HIGHEST, preferred_element_type=F32) * LOG2E


def _pool_prefix(pool):
    rows = pool.shape[0]
    return pl.pallas_call(
        _pool_prefix_kernel,
        grid=(rows // POOL_TILE,),
        in_specs=[pl.BlockSpec((POOL_TILE, PAGE), lambda i: (i, 0))],
        out_specs=pl.BlockSpec((POOL_TILE, PAGE), lambda i: (i, 0)),
        out_shape=jax.ShapeDtypeStruct(pool.shape, F32),
        compiler_params=_params(("parallel",)),
        name="pool_prefix",
    )(pool)


def _attn_p_kernel(q_ref, k_ref, v_ref, c_ref, ct_ref, o_ref, m_ref, acc_ref, cq_ref, vx_ref):
    t = ATT_TILE
    n_heads = 2 * ATT_COLS
    head0 = pl.program_id(1) * n_heads
    qi = pl.program_id(2)
    lane = lax.broadcasted_iota(jnp.int32, (t, LANES), 1)
    first = lane < HEAD_DIM
    cols = lambda c: slice(c * LANES, (c + 1) * LANES)

    @pl.when(qi == 0)
    def _():
        own = lax.broadcasted_iota(jnp.int32, (v_ref.shape[0], LANES), 1) < HEAD_DIM
        for c in range(ATT_COLS):
            v = v_ref[:, cols(c)]
            one = jnp.ones_like(v)
            vx_ref[2 * c] = jnp.where(own, v, one)
            vx_ref[2 * c + 1] = jnp.where(own, one, v)

    q_heads = []
    for c in range(ATT_COLS):
        q2 = q_ref[:, cols(c)]
        zero = jnp.zeros_like(q2)
        q_heads += [jnp.where(first, q2, zero), jnp.where(first, zero, q2)]
    crow = c_ref[...]
    hlane = lax.broadcasted_iota(jnp.int32, crow.shape, 1)
    for e in range(n_heads):
        cq = jnp.sum(jnp.where(hlane == head0 + e, crow, 0.0), axis=-1, keepdims=True)
        cq_ref[e] = jnp.broadcast_to(cq, (t, LANES))
    m_ref[...] = jnp.full(m_ref.shape, NEG, F32)
    acc_ref[...] = jnp.zeros(acc_ref.shape, F32)
    def block(r0, nr, start, width, row_shift):
        rs = slice(r0, r0 + nr)
        if row_shift is not None:
            row = lax.broadcasted_iota(jnp.int32, (nr, width), 0)
            col = lax.broadcasted_iota(jnp.int32, (nr, width), 1)
            visible = col <= row + row_shift
        for e in range(n_heads):
            kc = k_ref[pl.ds(start, width), cols(e // 2)]
            ck = ct_ref[pl.ds(head0 + e, 1), pl.ds(start, width)]
            u = _dot_nt(q_heads[e][rs], kc) - ck
            if row_shift is not None:
                u = jnp.where(visible, u, NEG)
            cq = cq_ref[e, rs]
            m_prev = m_ref[e, rs]
            m_new = jnp.maximum(m_prev, jnp.max(u, axis=-1, keepdims=True) + cq)
            alpha = jnp.exp2(m_prev - m_new)
            p = jnp.exp2(u + jnp.concatenate([cq - m_new] * (width // LANES), axis=1))
            acc_ref[e, rs] = alpha * acc_ref[e, rs] + _dot(p.astype(BF16), vx_ref[e, pl.ds(start, width), :])
            m_ref[e, rs] = m_new

    kt = ATT_KTILE

    def body(j, carry):
        block(0, t, pl.multiple_of(j * kt, kt), kt, None)
        return carry

    lax.fori_loop(0, qi * (t // kt), body, 0)
    block(0, t, pl.multiple_of(qi * t, t), t, 0)
    for c in range(ATT_COLS):
        a0 = acc_ref[2 * c]
        a1 = acc_ref[2 * c + 1]
        o_ref[:, cols(c)] = jnp.where(first, a0 / pltpu.roll(a0, HEAD_DIM, 1), a1 / pltpu.roll(a1, HEAD_DIM, 1))


def _attn_p(qb, kb, vb, c_rows, ct, batch, seq):
    t = ATT_TILE
    nq = seq // t
    w = ATT_COLS * LANES
    return pl.pallas_call(
        _attn_p_kernel,
        grid=(batch, WIDTH_A // w, nq),
        in_specs=[
            pl.BlockSpec((t, w), lambda b, h, i: (b * nq + i, h)),
            pl.BlockSpec((seq, w), lambda b, h, i: (b, h)),
            pl.BlockSpec((seq, w), lambda b, h, i: (b, h)),
            pl.BlockSpec((t, N_HEADS), lambda b, h, i: (b * nq + i, 0)),
            pl.BlockSpec((N_HEADS, seq), lambda b, h, i: (0, b)),
        ],
        out_specs=pl.BlockSpec((t, w), lambda b, h, i: (b * nq + i, h)),
        out_shape=jax.ShapeDtypeStruct((qb.shape[0], WIDTH_A), F32),
        scratch_shapes=[
            pltpu.VMEM((2 * ATT_COLS, t, LANES), F32),
            pltpu.VMEM((2 * ATT_COLS, t, LANES), F32),
            pltpu.VMEM((2 * ATT_COLS, t, LANES), F32),
            pltpu.VMEM((2 * ATT_COLS, seq, LANES), BF16),
        ],
        compiler_params=_params(("parallel", "parallel", "arbitrary")),
        name="attn_p",
    )(qb, kb, vb, c_rows, ct)


def _transpose_rows8(x):
    pad = jnp.concatenate([x, jnp.zeros((LANES - SUBLANES, LANES), F32)], axis=0)
    return pad.T[0:SUBLANES, :]


def _attn_s_kernel(pt_ref, q_ref, kn_ref, vn_ref, lfn_ref, ck_hbm, cv_hbm, cc_hbm, o_alias, o_ref,
                   kbuf, vbuf, cbuf, sem, *, n_pages, t_new, layer, n_seq):
    n = pl.program_id(0)
    slot = lax.rem(n, PAGE_SLOTS)
    ahead = jnp.minimum(n + PAGE_AHEAD, n_seq - 1)
    slot_ahead = lax.rem(n + PAGE_AHEAD, PAGE_SLOTS)

    def page_copies(seq, slot_, j):
        pg = pt_ref[seq, j]
        return (pltpu.make_async_copy(ck_hbm.at[pg, layer], kbuf.at[slot_, j], sem.at[0, slot_]),
                pltpu.make_async_copy(cv_hbm.at[pg, layer], vbuf.at[slot_, j], sem.at[1, slot_]),
                pltpu.make_async_copy(cc_hbm.at[pg, layer], cbuf.at[slot_, j], sem.at[2, slot_]))

    @pl.when(n == 0)
    def _():
        for s in range(PAGE_AHEAD):
            for j in range(n_pages):
                for c in page_copies(min(s, n_seq - 1), s, j):
                    c.start()

    for j in range(n_pages):
        for c in page_copies(n, slot, j):
            c.wait()
    k_pages, v_pages, c_pages = [], [], []
    for j in range(n_pages):
        k_pages.append(kbuf[slot, j].reshape(WIDTH_A, PAGE).astype(BF16))
        v_pages.append(vbuf[slot, j].reshape(WIDTH_A, PAGE).astype(BF16))
        c_pages.append(cbuf[slot, j])
        for c in page_copies(ahead, slot_ahead, j):
            c.start()

    rows = N_HEADS * t_new
    rhead = lax.broadcasted_iota(jnp.int32, (rows, WIDTH_A), 0) // t_new
    chead = lax.broadcasted_iota(jnp.int32, (rows, WIDTH_A), 1) // HEAD_DIM
    diag = rhead == chead
    q = q_ref[...]
    qbd = jnp.where(diag, jnp.concatenate([q] * N_HEADS, axis=0), 0.0).astype(BF16)

    def rep(ct):
        return jnp.broadcast_to(ct[:, None, :], (N_HEADS, t_new, LANES)).reshape(rows, LANES)

    carry = jnp.zeros((N_HEADS, 1), F32)
    cts = []
    for within in c_pages:
        cts.append(within + carry)
        carry = carry + within[:, PAGE - 1:PAGE]
    lft_new = _transpose_rows8(lfn_ref[...])
    ct_new = _lane_prefix(lft_new, t_new) * LOG2E + carry
    lane = lax.broadcasted_iota(jnp.int32, (rows, LANES), 1)
    tq = lax.broadcasted_iota(jnp.int32, (rows, LANES), 0) % t_new
    ct_new_rep = rep(ct_new)
    cq = jnp.sum(jnp.where(lane == tq, ct_new_rep, 0.0), axis=-1, keepdims=True)

    n_past = n_pages * PAGE
    k_past = jnp.concatenate(k_pages, axis=1)
    v_past = jnp.concatenate(v_pages, axis=1)
    ct_past = jnp.concatenate(cts, axis=1)
    ct_past_rep = jnp.broadcast_to(ct_past[:, None, :], (N_HEADS, t_new, n_past)).reshape(rows, n_past)
    u_past = _dot(qbd, k_past) - ct_past_rep
    zeros_pad = jnp.zeros((PAGE - t_new, WIDTH_A), F32)
    k_new = jnp.concatenate([kn_ref[...], zeros_pad], axis=0).astype(BF16)
    v_new = jnp.concatenate([vn_ref[...], zeros_pad], axis=0).astype(BF16)
    u_new = jnp.where(lane <= tq, _dot_nt(qbd, k_new) - ct_new_rep, NEG)

    m = jnp.maximum(jnp.max(u_past, axis=-1, keepdims=True), jnp.max(u_new, axis=-1, keepdims=True)) + cq
    p_past = jnp.exp2((u_past + cq) - m)
    p_new = jnp.exp2((u_new + cq) - m)
    l = jnp.sum(p_past, axis=-1, keepdims=True) + jnp.sum(p_new, axis=-1, keepdims=True)
    acc = _dot_nt(p_past.astype(BF16), v_past) + _dot(p_new.astype(BF16), v_new)
    o = jnp.where(diag, acc / l, 0.0)
    out = o[0:t_new, :]
    for h in range(1, N_HEADS):
        out = out + o[h * t_new:(h + 1) * t_new, :]
    o_ref[...] = out

    @pl.when(n == n_seq - 1)
    def _():
        for s in range(1, PAGE_AHEAD + 1):
            for j in range(n_pages):
                for c in page_copies(n, lax.rem(n + s, PAGE_SLOTS), j):
                    c.wait()


def _attn_s(page_table, q, k, v, lf, cache_k, cache_v, cache_c, oa, layer, row0, t_new):
    n_seq, n_pages = page_table.shape
    blk0 = row0 // t_new
    new_spec = lambda w_: pl.BlockSpec((t_new, w_), lambda n, pt: (n, 0))
    in_specs = [new_spec(WIDTH_A), new_spec(WIDTH_A), new_spec(WIDTH_A), new_spec(LANES)]
    in_specs += [pl.BlockSpec(memory_space=pl.ANY)] * 4
    grid_spec = pltpu.PrefetchScalarGridSpec(
        num_scalar_prefetch=1,
        grid=(n_seq,),
        in_specs=in_specs,
        out_specs=pl.BlockSpec((t_new, WIDTH_A), lambda n, pt: (blk0 + n, 0)),
        scratch_shapes=[
            pltpu.VMEM((PAGE_SLOTS, n_pages, N_HEADS, HEAD_DIM, PAGE), F32),
            pltpu.VMEM((PAGE_SLOTS, n_pages, N_HEADS, HEAD_DIM, PAGE), F32),
            pltpu.VMEM((PAGE_SLOTS, n_pages, N_HEADS, PAGE), F32),
            pltpu.SemaphoreType.DMA((3, PAGE_SLOTS)),
        ],
    )
    return pl.pallas_call(
        functools.partial(_attn_s_kernel, n_pages=n_pages, t_new=t_new, layer=layer, n_seq=n_seq),
        grid_spec=grid_spec,
        out_shape=jax.ShapeDtypeStruct(oa.shape, oa.dtype),
        input_output_aliases={len(in_specs): 0},
        compiler_params=_params(("arbitrary",)),
        name="attn_s",
    )(page_table, q, k, v, lf, cache_k, cache_v, cache_c, oa)


def _lru_coeffs(xc, wa_ref, ba_ref, wx_ref, bx_ref, lam_ref):
    xb = xc.astype(BF16)
    r = jax.nn.sigmoid(_dot(xb, wa_ref[...]) + ba_ref[...])
    ig = jax.nn.sigmoid(_dot(xb, wx_ref[...]) + bx_ref[...])
    log_a = (-LRU_C * r) * _softplus(-lam_ref[...])
    a = jnp.exp(log_a)
    b = jnp.sqrt(1.0 - a * a) * (ig * xc)
    return a, b


def _mix_p_kernel(z_ref, cw_ref, cb_ref, wa_ref, ba_ref, wx_ref, bx_ref, lam_ref, sw_ref,
                  ob_ref, oc_ref, h_ref, lc_ref, scc_ref,
                  ubuf, pbuf, a_s, b_s, hcar):
    tt = MIX_TILE
    hist = SUBLANES
    i = pl.program_id(1)

    @pl.when(i == 0)
    def _():
        ubuf[0:hist, :] = jnp.zeros((hist, WIDTH_B), F32)
        pbuf[0:hist, :] = jnp.zeros((hist, WIDTH_C), F32)
        hcar[...] = jnp.zeros(hcar.shape, F32)

    u = z_ref[:, 0:WIDTH_B]
    ubuf[hist:hist + tt, :] = u
    xc = cw_ref[0:1, :] * ubuf[hist - 3:hist - 3 + tt, :]
    xc = xc + cw_ref[1:2, :] * ubuf[hist - 2:hist - 2 + tt, :]
    xc = xc + cw_ref[2:3, :] * ubuf[hist - 1:hist - 1 + tt, :]
    xc = xc + cw_ref[3:4, :] * u
    xc = xc + cb_ref[...]
    ubuf[0:hist, :] = ubuf[tt:tt + hist, :]
    lc_ref[...] = z_ref[tt - (CONV_B - 1):tt, 0:WIDTH_B]

    a, b = _lru_coeffs(xc, wa_ref, ba_ref, wx_ref, bx_ref, lam_ref)
    a_s[...] = a
    b_s[...] = b
    srow = lax.broadcasted_iota(jnp.int32, (SUBLANES, WIDTH_B), 0)

    def chunk(c, h):
        r0 = pl.multiple_of(c * SUBLANES, SUBLANES)
        aa = a_s[pl.ds(r0, SUBLANES), :]
        bb = b_s[pl.ds(r0, SUBLANES), :]
        for s in (1, 2, 4):
            keep = srow >= s
            a_sh = jnp.where(keep, pltpu.roll(aa, s, 0), 1.0)
            b_sh = jnp.where(keep, pltpu.roll(bb, s, 0), 0.0)
            bb = aa * b_sh + bb
            aa = aa * a_sh
        hc = aa * h + bb
        b_s[pl.ds(r0, SUBLANES), :] = hc
        return hc[SUBLANES - 1:SUBLANES, :]

    h_last = lax.fori_loop(0, tt // SUBLANES, chunk, hcar[...], unroll=4)
    hcar[...] = h_last
    h_ref[...] = h_last
    ob_ref[...] = b_s[...] * jax.nn.gelu(z_ref[:, WIDTH_B:2 * WIDTH_B])

    o3 = 2 * WIDTH_B
    pc = z_ref[:, o3 + WIDTH_C:o3 + 2 * WIDTH_C] * z_ref[:, o3 + 2 * WIDTH_C:o3 + 3 * WIDTH_C]
    pbuf[hist:hist + tt, :] = pc
    uc = sw_ref[0:1, :] * pbuf[hist - 2:hist - 2 + tt, :]
    uc = uc + sw_ref[1:2, :] * pbuf[hist - 1:hist - 1 + tt, :]
    uc = uc + sw_ref[2:3, :] * pc
    pbuf[0:hist, :] = pbuf[tt:tt + hist, :]
    scc_ref[...] = pbuf[hist + tt - (CONV_C - 1):hist + tt, :]
    oc_ref[...] = z_ref[:, o3:o3 + WIDTH_C] * uc


def _mix_weight_specs(layer):
    def im(*_):
        return (layer, 0, 0)

    return [
        pl.BlockSpec((None, CONV_B, WIDTH_B), im),
        pl.BlockSpec((None, 1, WIDTH_B), im),
        pl.BlockSpec((None, WIDTH_B, WIDTH_B), im),
        pl.BlockSpec((None, 1, WIDTH_B), im),
        pl.BlockSpec((None, WIDTH_B, WIDTH_B), im),
        pl.BlockSpec((None, 1, WIDTH_B), im),
        pl.BlockSpec((None, 1, WIDTH_B), im),
        pl.BlockSpec((None, CONV_C, WIDTH_C), im),
    ]


def _mix_p(zmix, weights, layer, batch, seq):
    tt = MIX_TILE
    nt = seq // tt
    rows = zmix.shape[0]
    return pl.pallas_call(
        _mix_p_kernel,
        grid=(batch, nt),
        in_specs=[pl.BlockSpec((tt, W_MIX), lambda b, i: (b * nt + i, 0))] + _mix_weight_specs(layer),
        out_specs=[
            pl.BlockSpec((tt, WIDTH_B), lambda b, i: (b * nt + i, 0)),
            pl.BlockSpec((tt, WIDTH_C), lambda b, i: (b * nt + i, 0)),
            pl.BlockSpec((None, 1, WIDTH_B), lambda b, i: (b, 0, 0)),
            pl.BlockSpec((None, CONV_B - 1, WIDTH_B), lambda b, i: (b, 0, 0)),
            pl.BlockSpec((None, CONV_C - 1, WIDTH_C), lambda b, i: (b, 0, 0)),
        ],
        out_shape=[
            jax.ShapeDtypeStruct((rows, WIDTH_B), F32),
            jax.ShapeDtypeStruct((rows, WIDTH_C), F32),
            jax.ShapeDtypeStruct((batch, 1, WIDTH_B), F32),
            jax.ShapeDtypeStruct((batch, CONV_B - 1, WIDTH_B), F32),
            jax.ShapeDtypeStruct((batch, CONV_C - 1, WIDTH_C), F32),
        ],
        scratch_shapes=[
            pltpu.VMEM((tt + SUBLANES, WIDTH_B), F32),
            pltpu.VMEM((tt + SUBLANES, WIDTH_C), F32),
            pltpu.VMEM((tt, WIDTH_B), F32),
            pltpu.VMEM((tt, WIDTH_B), F32),
            pltpu.VMEM((1, WIDTH_B), F32),
        ],
        compiler_params=_params(("arbitrary", "arbitrary")),
        name="mix_p",
    )(zmix, *weights)


def _mix_s_kernel(z_ref, h0_ref, lch_ref, sch_ref, cw_ref, cb_ref, wa_ref, ba_ref, wx_ref, bx_ref,
                  lam_ref, sw_ref, ob_alias, oc_alias, ob_ref, oc_ref, hs_ref, pc_ref, *, t_new):
    tt = z_ref.shape[0]
    o3 = 2 * WIDTH_B
    tpos = lax.broadcasted_iota(jnp.int32, (tt, WIDTH_B), 0) % t_new

    def back(x, hist, s):
        return jnp.where(tpos >= s, pltpu.roll(x, s, 0), pltpu.roll(hist, tt - (t_new - s), 0))

    u = z_ref[:, 0:WIDTH_B]
    lch = lch_ref[...]
    xc = cw_ref[0:1, :] * back(u, lch, 3)
    xc = xc + cw_ref[1:2, :] * back(u, lch, 2)
    xc = xc + cw_ref[2:3, :] * back(u, lch, 1)
    xc = xc + cw_ref[3:4, :] * u
    xc = xc + cb_ref[...]
    a, b = _lru_coeffs(xc, wa_ref, ba_ref, wx_ref, bx_ref, lam_ref)
    b = b + a * h0_ref[...]
    for s in (1, 2, 4):
        keep = tpos >= s
        a_sh = jnp.where(keep, pltpu.roll(a, s, 0), 1.0)
        b_sh = jnp.where(keep, pltpu.roll(b, s, 0), 0.0)
        b = a * b_sh + b
        a = a * a_sh
    hs_ref[...] = b
    ob_ref[...] = b * jax.nn.gelu(z_ref[:, WIDTH_B:2 * WIDTH_B])

    pc = z_ref[:, o3 + WIDTH_C:o3 + 2 * WIDTH_C] * z_ref[:, o3 + 2 * WIDTH_C:o3 + 3 * WIDTH_C]
    sch = sch_ref[...]
    uc = sw_ref[0:1, :] * back(pc, sch, 2)
    uc = uc + sw_ref[1:2, :] * back(pc, sch, 1)
    uc = uc + sw_ref[2:3, :] * pc
    pc_ref[...] = pc
    oc_ref[...] = z_ref[:, o3:o3 + WIDTH_C] * uc


def _mix_s(zmix, h0e, lch, sch, weights, ob, oc, layer, row0, n_seq, t_new):
    ns = SEQ_PER_MIX_STEP
    tt = ns * t_new
    blk0 = row0 // tt
    rows = n_seq * t_new
    row_spec = lambda w_: pl.BlockSpec((tt, w_), lambda i: (i, 0))
    all_spec = lambda w_: pl.BlockSpec((tt, w_), lambda i: (blk0 + i, 0))
    in_specs = [all_spec(W_MIX), row_spec(WIDTH_B), row_spec(WIDTH_B), row_spec(WIDTH_C)]
    in_specs += _mix_weight_specs(layer)
    in_specs += [pl.BlockSpec(memory_space=pl.ANY)] * 2
    return pl.pallas_call(
        functools.partial(_mix_s_kernel, t_new=t_new),
        grid=(n_seq // ns,),
        in_specs=in_specs,
        out_specs=[all_spec(WIDTH_B), all_spec(WIDTH_C), row_spec(WIDTH_B), row_spec(WIDTH_C)],
        out_shape=[
            jax.ShapeDtypeStruct(ob.shape, ob.dtype),
            jax.ShapeDtypeStruct(oc.shape, oc.dtype),
            jax.ShapeDtypeStruct((rows, WIDTH_B), F32),
            jax.ShapeDtypeStruct((rows, WIDTH_C), F32),
        ],
        input_output_aliases={len(in_specs) - 2: 0, len(in_specs) - 1: 1},
        compiler_params=_params(("parallel",)),
        name="mix_s",
    )(zmix, h0e, lch, sch, *weights, ob, oc)


def _merge_kernel(x_ref, oa_ref, ob_ref, oc_ref, g_ref, wb_ref, wo_ref, o_ref):
    merged = None
    lo = 0
    for b, (o_b, width) in enumerate(((oa_ref, WIDTH_A), (ob_ref, WIDTH_B), (oc_ref, WIDTH_C))):
        proj = _dot(o_b[...].astype(BF16), wb_ref[lo:lo + width, :])
        term = g_ref[:, b * D_MODEL:(b + 1) * D_MODEL].astype(F32) * proj
        merged = term if merged is None else merged + term
        lo += width
    o_ref[...] = x_ref[...] + _dot(merged.astype(BF16), wo_ref[...])


def _merge(x, oa, ob, oc, g, wb, wo, layer):
    rows = x.shape[0]
    row_spec = lambda w_: pl.BlockSpec((ROW_TILE, w_), lambda i: (i, 0))
    return pl.pallas_call(
        _merge_kernel,
        grid=(rows // ROW_TILE,),
        in_specs=[
            row_spec(D_MODEL), row_spec(WIDTH_A), row_spec(WIDTH_B), row_spec(WIDTH_C), row_spec(W_G),
            _resident((None, WIDTH_A + WIDTH_B + WIDTH_C, D_MODEL), lambda i: (layer, 0, 0)),
            _resident((None, D_MODEL, D_MODEL), lambda i: (layer, 0, 0)),
        ],
        out_specs=row_spec(D_MODEL),
        out_shape=jax.ShapeDtypeStruct((rows, D_MODEL), F32),
        compiler_params=_params(("parallel",)),
        name="merge",
    )(x, oa, ob, oc, g, wb, wo)


def kernel(x_prompt, x_sample, cache_k, cache_v, cache_logf, state_lru_h, state_lru_conv, state_sc_conv,
           page_table, ffn1_norm, ffn1_w_in, ffn1_w_out, mix_norm, w_in, b_forget, q_norm, k_norm,
           lru_conv_w, lru_conv_b, lru_wa, lru_ba, lru_wx, lru_bx, lru_lambda, sc_conv_w, w_branch, w_out,
           ffn2_norm, ffn2_w_in, ffn2_w_out):
    batch, seq, _ = x_prompt.shape
    n_seq, t_new, _ = x_sample.shape
    depth = w_in.shape[0]
    rows_p = batch * seq
    rows_s = n_seq * t_new

    vec = lambda a: a[:, None, :]
    w1a, w2a = ffn1_w_in.astype(BF16), ffn1_w_out.astype(BF16)
    w1b, w2b = ffn2_w_in.astype(BF16), ffn2_w_out.astype(BF16)
    w_in_p = (w_in[:, :, :OFF_F].astype(BF16), w_in[:, :, OFF_MIX:].astype(BF16),
              jnp.pad(w_in[:, :, OFF_F:OFF_MIX], ((0, 0), (0, 0), (0, LANES - N_HEADS))).astype(BF16))
    bf_p = jnp.pad(b_forget, ((0, 0), (0, LANES - N_HEADS)))[:, None, :]
    qn_p = vec(jnp.tile(q_norm, (1, N_HEADS)))
    kn_p = vec(jnp.tile(k_norm, (1, N_HEADS)))
    eye = jnp.eye(N_BLOCKS_B, dtype=lru_wa.dtype)
    dense = lambda w: jnp.einsum('lgij,gh->lgihj', w, eye).reshape(depth, WIDTH_B, WIDTH_B).astype(BF16)
    mix_weights = (lru_conv_w, vec(lru_conv_b), dense(lru_wa), vec(lru_ba), dense(lru_wx), vec(lru_bx),
                   vec(lru_lambda), sc_conv_w)
    wb_b, wo_b = w_branch.astype(BF16), w_out.astype(BF16)
    ck4 = jnp.transpose(cache_k, (0, 1, 3, 4, 2))
    cv4 = jnp.transpose(cache_v, (0, 1, 3, 4, 2))
    clft = jnp.swapaxes(cache_logf, 2, 3)
    cpool = _pool_prefix(clft.reshape(-1, PAGE)).reshape(clft.shape)

    def seq_rows(state, at_end):
        r = state.shape[1]
        pad = (t_new - r, 0) if at_end else (0, t_new - r)
        return jnp.pad(state, ((0, 0), pad, (0, 0))).reshape(rows_s, state.shape[-1])

    per_seq = lambda a: a.reshape(n_seq, t_new, a.shape[-1])
    x = (x_prompt.reshape(rows_p, D_MODEL), x_sample.reshape(rows_s, D_MODEL))
    prev = None
    outs_p = [[] for _ in range(3)]
    outs_s = [[] for _ in range(6)]
    for l in range(depth):
        x = _ffn(x, vec(ffn1_norm), w1a, w2a, l, rows_p, rows_s, False)
        (qb, kb, vb, qs, ks, vs, lfs, kt, vt, lft, zmix, gates) = _inproj(
            x, vec(mix_norm), w_in_p, bf_p, qn_p, kn_p, l, depth, batch, seq, rows_s, prev)
        prev = (kt, vt, lft)
        ct = _cumsum(lft, l, batch, seq)
        oa = _attn_p(qb, kb, vb, ct.T, ct, batch, seq)
        oa = _attn_s(page_table, qs, ks, vs, lfs, ck4, cv4, cpool, oa, l, rows_p, t_new)
        ob, oc, h_p, lc_p, sc_p = _mix_p(zmix, mix_weights, l, batch, seq)
        ob, oc, hs_s, pc_s = _mix_s(zmix, seq_rows(state_lru_h[:, l][:, None, :], False),
                                    seq_rows(state_lru_conv[:, l], True),
                                    seq_rows(state_sc_conv[:, l], True), mix_weights, ob, oc,
                                    l, rows_p, n_seq, t_new)
        x = _merge(x, oa, ob, oc, gates, wb_b, wo_b, l)
        x = _ffn(x, vec(ffn2_norm), w1b, w2b, l, rows_p, rows_s, l == depth - 1)
        for dst, val in zip(outs_p, (h_p[:, 0], lc_p, sc_p)):
            dst.append(val)
        for dst, val in zip(outs_s, (ks.reshape(n_seq, t_new, N_HEADS, HEAD_DIM),
                                     vs.reshape(n_seq, t_new, N_HEADS, HEAD_DIM),
                                     per_seq(lfs[:, :N_HEADS]),
                                     per_seq(hs_s)[:, -1],
                                     per_seq(zmix[rows_p:, :WIDTH_B])[:, t_new - (CONV_B - 1):],
                                     per_seq(pc_s)[:, t_new - (CONV_C - 1):])):
            dst.append(val)
    stack = lambda xs: jnp.stack(xs, axis=1)
    kt, vt, lft = prev
    heads = lambda a: jnp.transpose(a.reshape(batch, depth, N_HEADS, HEAD_DIM, seq), (0, 1, 4, 2, 3))
    y_p, y_s = x
    return ((y_p.reshape(batch, seq, D_MODEL), y_s.reshape(n_seq, t_new, D_MODEL),
             heads(kt), heads(vt), jnp.swapaxes(lft, 2, 3))
            + tuple(stack(o) for o in outs_p) + tuple(stack(o) for o in outs_s))
```

```python
import functools
import math

import jax
import jax.numpy as jnp
from jax import lax
from jax.experimental import pallas as pl
from jax.experimental.pallas import tpu as pltpu

F32 = jnp.float32
BF16 = jnp.bfloat16

D_MODEL = 1024
N_HEADS = 8
HEAD_DIM = 64
WIDTH_A = N_HEADS * HEAD_DIM
WIDTH_B = 512
N_BLOCKS_B = 8
CONV_B = 4
LRU_C = 8.0
WIDTH_C = 512
CONV_C = 3
N_BRANCH = 3
FFN_HIDDEN = 2816
RMS_EPS = 1e-6
PAGE = 128

LANES = 128
SUBLANES = 8
VMEM_LIMIT = 56 * 1024 * 1024
LOG2E = math.log2(math.e)

OFF_F = 3 * WIDTH_A
OFF_MIX = OFF_F + N_HEADS
W_MIX = 2 * WIDTH_B + 3 * WIDTH_C
OFF_G = OFF_MIX + W_MIX
W_G = N_BRANCH * D_MODEL

NEG = -1e30
ROW_TILE = 512
MXU_TILE = 256
FFN_CHUNKS = ((0, 5 * MXU_TILE), (5 * MXU_TILE, FFN_HIDDEN))
ATT_TILE = 512
ATT_COLS = 4
ATT_KTILE = 256
MIX_TILE = 512
SEQ_PER_MIX_STEP = 64
POOL_TILE = 2560
CAST_BLOCK_BYTES = 6 * 1024 * 1024
PAGE_AHEAD = 2
PAGE_SLOTS = PAGE_AHEAD + 1


def _params(sem):
    return pltpu.CompilerParams(dimension_semantics=sem, vmem_limit_bytes=VMEM_LIMIT)


def _resident(shape, index_map):
    return pl.BlockSpec(shape, index_map, pipeline_mode=pl.Buffered(1))


def _rms(x, g):
    return (x * lax.rsqrt(jnp.mean(x * x, axis=-1, keepdims=True) + RMS_EPS)) * g


def _softplus(x):
    return jnp.maximum(x, 0.0) + jnp.log1p(jnp.exp(-jnp.abs(x)))


def _dot(a, b):
    return jnp.dot(a, b, preferred_element_type=F32)


def _dot_nt(a, b):
    return lax.dot_general(a, b, (((1,), (1,)), ((), ())), preferred_element_type=F32)


def _prompt_only(n_p):
    return lambda i: (jnp.minimum(i, n_p - 1), 0)


def _sample_only(n_p):
    return lambda i: (jnp.maximum(i - n_p, 0), 0)


def _cast_kernel(x_ref, o_ref):
    o_ref[...] = x_ref[...].astype(BF16)


def _to_bf16(w):
    depth, r, c = w.shape
    n_blk = 1
    while (r // n_blk) * c * 4 > CAST_BLOCK_BYTES or r % n_blk or (r // n_blk) % (2 * SUBLANES):
        n_blk += 1
    spec = pl.BlockSpec((None, r // n_blk, c), lambda d, i: (d, i, 0))
    return pl.pallas_call(
        _cast_kernel,
        grid=(depth, n_blk),
        in_specs=[spec],
        out_specs=spec,
        out_shape=jax.ShapeDtypeStruct(w.shape, BF16),
        compiler_params=_params(("parallel", "parallel")),
        name="to_bf16",
    )(w)


def _ffn_kernel(*refs, n_p, split_in, split_out):
    i = pl.program_id(0)
    n_x = 2 if split_in else 1
    g_ref, w1_ref, w2_ref = refs[n_x:n_x + 3]
    outs = refs[n_x + 3:]
    if split_in:
        x = jnp.where(i < n_p, refs[0][...], refs[1][...])
    else:
        x = refs[0][...]
    xn = _rms(x, g_ref[...]).astype(BF16)
    acc = jnp.zeros(x.shape, F32)
    for lo, hi in FFN_CHUNKS:
        gate = _dot(xn, w1_ref[:, lo:hi])
        up = _dot(xn, w1_ref[:, FFN_HIDDEN + lo:FFN_HIDDEN + hi])
        act = (gate * jax.nn.sigmoid(gate)) * up
        acc = acc + _dot(act.astype(BF16), w2_ref[lo:hi, :])
    y = x + 0.5 * acc
    if split_out:
        @pl.when(i < n_p)
        def _():
            outs[0][...] = y

        @pl.when(i >= n_p)
        def _():
            outs[1][...] = y
    else:
        outs[0][...] = y


def _ffn(xs, norm, w1, w2, layer, rows_p, rows_s, split_out):
    n_p = rows_p // ROW_TILE
    rows = rows_p + rows_s
    split_in = isinstance(xs, tuple)
    tile = (ROW_TILE, D_MODEL)
    pair = [pl.BlockSpec(tile, _prompt_only(n_p)), pl.BlockSpec(tile, _sample_only(n_p))]
    whole = [pl.BlockSpec(tile, lambda i: (i, 0))]
    if split_out:
        out_shape = [jax.ShapeDtypeStruct((rows_p, D_MODEL), F32), jax.ShapeDtypeStruct((rows_s, D_MODEL), F32)]
    else:
        out_shape = [jax.ShapeDtypeStruct((rows, D_MODEL), F32)]
    out = pl.pallas_call(
        functools.partial(_ffn_kernel, n_p=n_p, split_in=split_in, split_out=split_out),
        grid=(rows // ROW_TILE,),
        in_specs=(pair if split_in else whole) + [
            pl.BlockSpec((None, 1, D_MODEL), lambda i: (layer, 0, 0)),
            _resident((None, D_MODEL, 2 * FFN_HIDDEN), lambda i: (layer, 0, 0)),
            _resident((None, FFN_HIDDEN, D_MODEL), lambda i: (layer, 0, 0)),
        ],
        out_specs=pair if split_out else whole,
        out_shape=out_shape,
        compiler_params=_params(("arbitrary",)),
        name="ffn",
    )(*(xs if split_in else (xs,)), norm, w1, w2)
    return tuple(out) if split_out else out[0]


def _head_rms(x, gamma):
    m = x.shape[0]
    lane = lax.broadcasted_iota(jnp.int32, (m, LANES), 1)
    first = lane < HEAD_DIM
    outs = []
    for c in range(WIDTH_A // LANES):
        xc = x[:, c * LANES:(c + 1) * LANES]
        sq = xc * xc
        s_lo = jnp.sum(jnp.where(first, sq, 0.0), axis=-1, keepdims=True)
        s_hi = jnp.sum(jnp.where(first, 0.0, sq), axis=-1, keepdims=True)
        ms = jnp.where(first, s_lo, s_hi) * (1.0 / HEAD_DIM)
        outs.append((xc * lax.rsqrt(ms + RMS_EPS)) * gamma[:, c * LANES:(c + 1) * LANES])
    return jnp.concatenate(outs, axis=-1)


def _inproj_kernel(x_ref, g_ref, wa_ref, wm_ref, wf_ref, bf_ref, qn_ref, kn_ref, *rest, n_s, n_alias):
    (qb_ref, kb_ref, vb_ref, qs_ref, ks_ref, vs_ref, lfs_ref,
     kt_ref, vt_ref, lft_ref, mix_ref, gate_ref) = rest[n_alias:]
    i = pl.program_id(0)
    xn = _rms(x_ref[...], g_ref[...]).astype(BF16)
    qkv = _dot(xn, wa_ref[...])
    q = _head_rms(qkv[:, 0:WIDTH_A], qn_ref[...]) * (HEAD_DIM ** -0.5 * LOG2E)
    k = _head_rms(qkv[:, WIDTH_A:2 * WIDTH_A], kn_ref[...])
    v = qkv[:, 2 * WIDTH_A:3 * WIDTH_A]
    qb_ref[...] = q.astype(BF16)
    kb_ref[...] = k.astype(BF16)
    vb_ref[...] = v.astype(BF16)
    mix_ref[...] = _dot(xn, wm_ref[:, 0:W_MIX])
    gate_ref[...] = jax.nn.sigmoid(_dot(xn, wm_ref[:, W_MIX:W_MIX + W_G])).astype(BF16)
    f = _dot(xn, wf_ref[...]) + bf_ref[...]
    lf = jnp.minimum(f, 0.0) - jnp.log1p(jnp.exp(-jnp.abs(f)))

    kt_ref[...] = k.T
    vt_ref[...] = v.T
    lft_ref[...] = lf.T[0:N_HEADS, :]

    @pl.when(i < n_s)
    def _():
        qs_ref[...] = q
        ks_ref[...] = k
        vs_ref[...] = v
        lfs_ref[...] = lf


def _inproj(x, norm, w, bf, qn, kn, layer, depth, batch, seq, rows_s, prev):
    rows = x.shape[0]
    n_p = batch * seq // ROW_TILE
    n_s = rows_s // ROW_TILE
    per_seq = seq // ROW_TILE
    tile = lambda i: jnp.where(i < n_s, n_p + i, i - n_s)
    row_spec = lambda w_: pl.BlockSpec((ROW_TILE, w_), lambda i: (tile(i), 0))
    smp_spec = lambda w_: pl.BlockSpec((ROW_TILE, w_), lambda i: (jnp.minimum(i, n_s - 1), 0))
    vec_spec = lambda w_: pl.BlockSpec((None, 1, w_), lambda i: (layer, 0, 0))

    def state_spec(r):
        def im(i):
            j = jnp.maximum(i - n_s, 0)
            return (j // per_seq, layer, 0, j % per_seq)
        return pl.BlockSpec((None, None, r, ROW_TILE), im)

    n_alias = 0 if prev is None else 3
    w_spec = lambda w_: _resident((None, D_MODEL, w_), lambda i: (layer, 0, 0))
    in_specs = [row_spec(D_MODEL), vec_spec(D_MODEL), w_spec(3 * WIDTH_A), w_spec(W_MIX + W_G), w_spec(LANES),
                vec_spec(LANES), vec_spec(WIDTH_A), vec_spec(WIDTH_A)]
    n_in = len(in_specs)
    in_specs += [pl.BlockSpec(memory_space=pl.ANY)] * n_alias
    sds = jax.ShapeDtypeStruct
    return pl.pallas_call(
        functools.partial(_inproj_kernel, n_s=n_s, n_alias=n_alias),
        grid=(rows // ROW_TILE,),
        in_specs=in_specs,
        out_specs=[
            row_spec(WIDTH_A), row_spec(WIDTH_A), row_spec(WIDTH_A),
            smp_spec(WIDTH_A), smp_spec(WIDTH_A), smp_spec(WIDTH_A), smp_spec(LANES),
            state_spec(WIDTH_A), state_spec(WIDTH_A), state_spec(N_HEADS),
            row_spec(W_MIX), row_spec(W_G),
        ],
        out_shape=[
            sds((rows, WIDTH_A), BF16), sds((rows, WIDTH_A), BF16), sds((rows, WIDTH_A), BF16),
            sds((rows_s, WIDTH_A), F32), sds((rows_s, WIDTH_A), F32), sds((rows_s, WIDTH_A), F32),
            sds((rows_s, LANES), F32),
            sds((batch, depth, WIDTH_A, seq), F32), sds((batch, depth, WIDTH_A, seq), F32),
            sds((batch, depth, N_HEADS, seq), F32),
            sds((rows, W_MIX), F32), sds((rows, W_G), BF16),
        ],
        input_output_aliases={n_in + a: 7 + a for a in range(n_alias)},
        compiler_params=_params(("arbitrary",)),
        name="inproj",
    )(x, norm, *w, bf, qn, kn, *(prev or ()))


def _lane_prefix(x, n_valid):
    lane = lax.broadcasted_iota(jnp.int32, x.shape, 1)
    shift = 1
    while shift < n_valid:
        x = x + jnp.where(lane >= shift, pltpu.roll(x, shift, 1), 0.0)
        shift *= 2
    return x


def _cumsum_kernel(x_ref, o_ref):
    o_ref[...] = _lane_prefix(x_ref[...], x_ref.shape[1]) * LOG2E


def _cumsum(lft, layer, batch, seq):
    return pl.pallas_call(
        _cumsum_kernel,
        grid=(batch,),
        in_specs=[pl.BlockSpec((None, None, N_HEADS, seq), lambda b: (b, layer, 0, 0))],
        out_specs=pl.BlockSpec((N_HEADS, seq), lambda b: (0, b)),
        out_shape=jax.ShapeDtypeStruct((N_HEADS, batch * seq), F32),
        compiler_params=_params(("parallel",)),
        name="cumsum",
    )(lft)


def _pool_prefix_kernel(x_ref, o_ref):
    r = lax.broadcasted_iota(jnp.int32, (PAGE, PAGE), 0)
    c = lax.broadcasted_iota(jnp.int32, (PAGE, PAGE), 1)
    tri = jnp.where(r <= c, 1.0, 0.0).astype(F32)
    o_ref[...] = jnp.dot(x_ref[...], tri, precision=lax.Precision.HIGHEST, preferred_element_type=F32) * LOG2E


def _pool_prefix(pool):
    rows = pool.shape[0]
    return pl.pallas_call(
        _pool_prefix_kernel,
        grid=(rows // POOL_TILE,),
        in_specs=[pl.BlockSpec((POOL_TILE, PAGE), lambda i: (i, 0))],
        out_specs=pl.BlockSpec((POOL_TILE, PAGE), lambda i: (i, 0)),
        out_shape=jax.ShapeDtypeStruct(pool.shape, F32),
        compiler_params=_params(("parallel",)),
        name="pool_prefix",
    )(pool)


def _attn_p_kernel(q_ref, k_ref, v_ref, c_ref, ct_ref, o_ref, m_ref, acc_ref, cq_ref, vx_ref):
    t = ATT_TILE
    n_heads = 2 * ATT_COLS
    head0 = pl.program_id(1) * n_heads
    qi = pl.program_id(2)
    lane = lax.broadcasted_iota(jnp.int32, (t, LANES), 1)
    first = lane < HEAD_DIM
    cols = lambda c: slice(c * LANES, (c + 1) * LANES)

    @pl.when(qi == 0)
    def _():
        own = lax.broadcasted_iota(jnp.int32, (v_ref.shape[0], LANES), 1) < HEAD_DIM
        for c in range(ATT_COLS):
            v = v_ref[:, cols(c)]
            one = jnp.ones_like(v)
            vx_ref[2 * c] = jnp.where(own, v, one)
            vx_ref[2 * c + 1] = jnp.where(own, one, v)

    q_heads = []
    for c in range(ATT_COLS):
        q2 = q_ref[:, cols(c)]
        zero = jnp.zeros_like(q2)
        q_heads += [jnp.where(first, q2, zero), jnp.where(first, zero, q2)]
    crow = c_ref[...]
    hlane = lax.broadcasted_iota(jnp.int32, crow.shape, 1)
    for e in range(n_heads):
        cq = jnp.sum(jnp.where(hlane == head0 + e, crow, 0.0), axis=-1, keepdims=True)
        cq_ref[e] = jnp.broadcast_to(cq, (t, LANES))
    m_ref[...] = jnp.full(m_ref.shape, NEG, F32)
    acc_ref[...] = jnp.zeros(acc_ref.shape, F32)
    def block(r0, nr, start, width, row_shift):
        rs = slice(r0, r0 + nr)
        if row_shift is not None:
            row = lax.broadcasted_iota(jnp.int32, (nr, width), 0)
            col = lax.broadcasted_iota(jnp.int32, (nr, width), 1)
            visible = col <= row + row_shift
        for e in range(n_heads):
            kc = k_ref[pl.ds(start, width), cols(e // 2)]
            ck = ct_ref[pl.ds(head0 + e, 1), pl.ds(start, width)]
            u = _dot_nt(q_heads[e][rs], kc) - ck
            if row_shift is not None:
                u = jnp.where(visible, u, NEG)
            cq = cq_ref[e, rs]
            m_prev = m_ref[e, rs]
            m_new = jnp.maximum(m_prev, jnp.max(u, axis=-1, keepdims=True) + cq)
            alpha = jnp.exp2(m_prev - m_new)
            p = jnp.exp2(u + jnp.concatenate([cq - m_new] * (width // LANES), axis=1))
            acc_ref[e, rs] = alpha * acc_ref[e, rs] + _dot(p.astype(BF16), vx_ref[e, pl.ds(start, width), :])
            m_ref[e, rs] = m_new

    kt = ATT_KTILE

    def body(j, carry):
        for s in range(t // kt):
            block(0, t, pl.multiple_of(j * t, t) + s * kt, kt, None)
        return carry

    lax.fori_loop(0, qi, body, 0)
    block(0, t, pl.multiple_of(qi * t, t), t, 0)
    for c in range(ATT_COLS):
        a0 = acc_ref[2 * c]
        a1 = acc_ref[2 * c + 1]
        o_ref[:, cols(c)] = jnp.where(first, a0 / pltpu.roll(a0, HEAD_DIM, 1), a1 / pltpu.roll(a1, HEAD_DIM, 1))


def _attn_p(qb, kb, vb, c_rows, ct, batch, seq):
    t = ATT_TILE
    nq = seq // t
    w = ATT_COLS * LANES
    return pl.pallas_call(
        _attn_p_kernel,
        grid=(batch, WIDTH_A // w, nq),
        in_specs=[
            pl.BlockSpec((t, w), lambda b, h, i: (b * nq + i, h)),
            pl.BlockSpec((seq, w), lambda b, h, i: (b, h)),
            pl.BlockSpec((seq, w), lambda b, h, i: (b, h)),
            pl.BlockSpec((t, N_HEADS), lambda b, h, i: (b * nq + i, 0)),
            pl.BlockSpec((N_HEADS, seq), lambda b, h, i: (0, b)),
        ],
        out_specs=pl.BlockSpec((t, w), lambda b, h, i: (b * nq + i, h)),
        out_shape=jax.ShapeDtypeStruct((qb.shape[0], WIDTH_A), F32),
        scratch_shapes=[
            pltpu.VMEM((2 * ATT_COLS, t, LANES), F32),
            pltpu.VMEM((2 * ATT_COLS, t, LANES), F32),
            pltpu.VMEM((2 * ATT_COLS, t, LANES), F32),
            pltpu.VMEM((2 * ATT_COLS, seq, LANES), BF16),
        ],
        compiler_params=_params(("parallel", "parallel", "arbitrary")),
        name="attn_p",
    )(qb, kb, vb, c_rows, ct)


def _transpose_rows8(x):
    pad = jnp.concatenate([x, jnp.zeros((LANES - SUBLANES, LANES), F32)], axis=0)
    return pad.T[0:SUBLANES, :]


def _attn_s_kernel(pt_ref, q_ref, kn_ref, vn_ref, lfn_ref, ck_hbm, cv_hbm, cc_hbm, o_alias, o_ref,
                   kbuf, vbuf, cbuf, sem, *, n_pages, t_new, layer, n_seq):
    n = pl.program_id(0)
    slot = lax.rem(n, PAGE_SLOTS)
    ahead = jnp.minimum(n + PAGE_AHEAD, n_seq - 1)
    slot_ahead = lax.rem(n + PAGE_AHEAD, PAGE_SLOTS)

    def page_copies(seq, slot_, j):
        pg = pt_ref[seq, j]
        return (pltpu.make_async_copy(ck_hbm.at[pg, layer], kbuf.at[slot_, j], sem.at[0, slot_]),
                pltpu.make_async_copy(cv_hbm.at[pg, layer], vbuf.at[slot_, j], sem.at[1, slot_]),
                pltpu.make_async_copy(cc_hbm.at[pg, layer], cbuf.at[slot_, j], sem.at[2, slot_]))

    @pl.when(n == 0)
    def _():
        for s in range(PAGE_AHEAD):
            for j in range(n_pages):
                for c in page_copies(min(s, n_seq - 1), s, j):
                    c.start()

    for j in range(n_pages):
        for c in page_copies(n, slot, j):
            c.wait()
    k_pages, v_pages, c_pages = [], [], []
    for j in range(n_pages):
        k_pages.append(kbuf[slot, j].reshape(WIDTH_A, PAGE).astype(BF16))
        v_pages.append(vbuf[slot, j].reshape(WIDTH_A, PAGE).astype(BF16))
        c_pages.append(cbuf[slot, j])
        for c in page_copies(ahead, slot_ahead, j):
            c.start()

    rows = N_HEADS * t_new
    rhead = lax.broadcasted_iota(jnp.int32, (rows, WIDTH_A), 0) // t_new
    chead = lax.broadcasted_iota(jnp.int32, (rows, WIDTH_A), 1) // HEAD_DIM
    diag = rhead == chead
    q = q_ref[...]
    qbd = jnp.where(diag, jnp.concatenate([q] * N_HEADS, axis=0), 0.0).astype(BF16)

    def rep(ct):
        return jnp.broadcast_to(ct[:, None, :], (N_HEADS, t_new, LANES)).reshape(rows, LANES)

    carry = jnp.zeros((N_HEADS, 1), F32)
    cts = []
    for within in c_pages:
        cts.append(within + carry)
        carry = carry + within[:, PAGE - 1:PAGE]
    lft_new = _transpose_rows8(lfn_ref[...])
    ct_new = _lane_prefix(lft_new, t_new) * LOG2E + carry
    lane = lax.broadcasted_iota(jnp.int32, (rows, LANES), 1)
    tq = lax.broadcasted_iota(jnp.int32, (rows, LANES), 0) % t_new
    ct_new_rep = rep(ct_new)
    cq = jnp.sum(jnp.where(lane == tq, ct_new_rep, 0.0), axis=-1, keepdims=True)

    n_past = n_pages * PAGE
    k_past = jnp.concatenate(k_pages, axis=1)
    v_past = jnp.concatenate(v_pages, axis=1)
    ct_past = jnp.concatenate(cts, axis=1)
    ct_past_rep = jnp.broadcast_to(ct_past[:, None, :], (N_HEADS, t_new, n_past)).reshape(rows, n_past)
    u_past = _dot(qbd, k_past) - ct_past_rep
    zeros_pad = jnp.zeros((PAGE - t_new, WIDTH_A), F32)
    k_new = jnp.concatenate([kn_ref[...], zeros_pad], axis=0).astype(BF16)
    v_new = jnp.concatenate([vn_ref[...], zeros_pad], axis=0).astype(BF16)
    u_new = jnp.where(lane <= tq, _dot_nt(qbd, k_new) - ct_new_rep, NEG)

    m = jnp.maximum(jnp.max(u_past, axis=-1, keepdims=True), jnp.max(u_new, axis=-1, keepdims=True)) + cq
    p_past = jnp.exp2((u_past + cq) - m)
    p_new = jnp.exp2((u_new + cq) - m)
    l = jnp.sum(p_past, axis=-1, keepdims=True) + jnp.sum(p_new, axis=-1, keepdims=True)
    acc = _dot_nt(p_past.astype(BF16), v_past) + _dot(p_new.astype(BF16), v_new)
    o = jnp.where(diag, acc / l, 0.0)
    out = o[0:t_new, :]
    for h in range(1, N_HEADS):
        out = out + o[h * t_new:(h + 1) * t_new, :]
    o_ref[...] = out

    @pl.when(n == n_seq - 1)
    def _():
        for s in range(1, PAGE_AHEAD + 1):
            for j in range(n_pages):
                for c in page_copies(n, lax.rem(n + s, PAGE_SLOTS), j):
                    c.wait()


def _attn_s(page_table, q, k, v, lf, cache_k, cache_v, cache_c, oa, layer, row0, t_new):
    n_seq, n_pages = page_table.shape
    blk0 = row0 // t_new
    new_spec = lambda w_: pl.BlockSpec((t_new, w_), lambda n, pt: (n, 0))
    in_specs = [new_spec(WIDTH_A), new_spec(WIDTH_A), new_spec(WIDTH_A), new_spec(LANES)]
    in_specs += [pl.BlockSpec(memory_space=pl.ANY)] * 4
    grid_spec = pltpu.PrefetchScalarGridSpec(
        num_scalar_prefetch=1,
        grid=(n_seq,),
        in_specs=in_specs,
        out_specs=pl.BlockSpec((t_new, WIDTH_A), lambda n, pt: (blk0 + n, 0)),
        scratch_shapes=[
            pltpu.VMEM((PAGE_SLOTS, n_pages, N_HEADS, HEAD_DIM, PAGE), F32),
            pltpu.VMEM((PAGE_SLOTS, n_pages, N_HEADS, HEAD_DIM, PAGE), F32),
            pltpu.VMEM((PAGE_SLOTS, n_pages, N_HEADS, PAGE), F32),
            pltpu.SemaphoreType.DMA((3, PAGE_SLOTS)),
        ],
    )
    return pl.pallas_call(
        functools.partial(_attn_s_kernel, n_pages=n_pages, t_new=t_new, layer=layer, n_seq=n_seq),
        grid_spec=grid_spec,
        out_shape=jax.ShapeDtypeStruct(oa.shape, oa.dtype),
        input_output_aliases={len(in_specs): 0},
        compiler_params=_params(("arbitrary",)),
        name="attn_s",
    )(page_table, q, k, v, lf, cache_k, cache_v, cache_c, oa)


def _lru_coeffs(xc, wa_ref, ba_ref, wx_ref, bx_ref, lam_ref):
    xb = xc.astype(BF16)
    sigmoid = lambda x: 0.5 * jnp.tanh(0.5 * x) + 0.5
    r = sigmoid(_dot(xb, wa_ref[...]) + ba_ref[...])
    ig = sigmoid(_dot(xb, wx_ref[...]) + bx_ref[...])
    log_a = (-LRU_C * r) * _softplus(-lam_ref[...])
    a = jnp.exp(log_a)
    y = 1.0 - a * a
    b = jnp.where(y > 0.0, y * lax.rsqrt(y), 0.0) * (ig * xc)
    return a, b


def _mix_p_kernel(z_ref, cw_ref, cb_ref, wa_ref, ba_ref, wx_ref, bx_ref, lam_ref, sw_ref,
                  ob_ref, oc_ref, h_ref, lc_ref, scc_ref,
                  ubuf, pbuf, a_s, b_s, hcar):
    tt = MIX_TILE
    hist = SUBLANES
    i = pl.program_id(1)

    @pl.when(i == 0)
    def _():
        ubuf[0:hist, :] = jnp.zeros((hist, WIDTH_B), F32)
        pbuf[0:hist, :] = jnp.zeros((hist, WIDTH_C), F32)
        hcar[...] = jnp.zeros(hcar.shape, F32)

    u = z_ref[:, 0:WIDTH_B]
    ubuf[hist:hist + tt, :] = u
    xc = cw_ref[0:1, :] * ubuf[hist - 3:hist - 3 + tt, :]
    xc = xc + cw_ref[1:2, :] * ubuf[hist - 2:hist - 2 + tt, :]
    xc = xc + cw_ref[2:3, :] * ubuf[hist - 1:hist - 1 + tt, :]
    xc = xc + cw_ref[3:4, :] * u
    xc = xc + cb_ref[...]
    ubuf[0:hist, :] = ubuf[tt:tt + hist, :]
    lc_ref[...] = z_ref[tt - (CONV_B - 1):tt, 0:WIDTH_B]

    a, b = _lru_coeffs(xc, wa_ref, ba_ref, wx_ref, bx_ref, lam_ref)
    a_s[...] = a
    b_s[...] = b
    srow = lax.broadcasted_iota(jnp.int32, (SUBLANES, WIDTH_B), 0)

    def chunk(c, h):
        r0 = pl.multiple_of(c * SUBLANES, SUBLANES)
        aa = a_s[pl.ds(r0, SUBLANES), :]
        bb = b_s[pl.ds(r0, SUBLANES), :]
        for s in (1, 2, 4):
            keep = srow >= s
            a_sh = jnp.where(keep, pltpu.roll(aa, s, 0), 1.0)
            b_sh = jnp.where(keep, pltpu.roll(bb, s, 0), 0.0)
            bb = aa * b_sh + bb
            aa = aa * a_sh
        hc = aa * h + bb
        b_s[pl.ds(r0, SUBLANES), :] = hc
        return hc[SUBLANES - 1:SUBLANES, :]

    h_last = lax.fori_loop(0, tt // SUBLANES, chunk, hcar[...], unroll=4)
    hcar[...] = h_last
    h_ref[...] = h_last
    ob_ref[...] = b_s[...] * jax.nn.gelu(z_ref[:, WIDTH_B:2 * WIDTH_B])

    o3 = 2 * WIDTH_B
    pc = z_ref[:, o3 + WIDTH_C:o3 + 2 * WIDTH_C] * z_ref[:, o3 + 2 * WIDTH_C:o3 + 3 * WIDTH_C]
    pbuf[hist:hist + tt, :] = pc
    uc = sw_ref[0:1, :] * pbuf[hist - 2:hist - 2 + tt, :]
    uc = uc + sw_ref[1:2, :] * pbuf[hist - 1:hist - 1 + tt, :]
    uc = uc + sw_ref[2:3, :] * pc
    pbuf[0:hist, :] = pbuf[tt:tt + hist, :]
    scc_ref[...] = pbuf[hist + tt - (CONV_C - 1):hist + tt, :]
    oc_ref[...] = z_ref[:, o3:o3 + WIDTH_C] * uc


def _mix_weight_specs(layer):
    def im(*_):
        return (layer, 0, 0)

    return [
        pl.BlockSpec((None, CONV_B, WIDTH_B), im),
        pl.BlockSpec((None, 1, WIDTH_B), im),
        pl.BlockSpec((None, WIDTH_B, WIDTH_B), im),
        pl.BlockSpec((None, 1, WIDTH_B), im),
        pl.BlockSpec((None, WIDTH_B, WIDTH_B), im),
        pl.BlockSpec((None, 1, WIDTH_B), im),
        pl.BlockSpec((None, 1, WIDTH_B), im),
        pl.BlockSpec((None, CONV_C, WIDTH_C), im),
    ]


def _mix_p(zmix, weights, layer, batch, seq):
    tt = MIX_TILE
    nt = seq // tt
    rows = zmix.shape[0]
    return pl.pallas_call(
        _mix_p_kernel,
        grid=(batch, nt),
        in_specs=[pl.BlockSpec((tt, W_MIX), lambda b, i: (b * nt + i, 0))] + _mix_weight_specs(layer),
        out_specs=[
            pl.BlockSpec((tt, WIDTH_B), lambda b, i: (b * nt + i, 0)),
            pl.BlockSpec((tt, WIDTH_C), lambda b, i: (b * nt + i, 0)),
            pl.BlockSpec((None, 1, WIDTH_B), lambda b, i: (b, 0, 0)),
            pl.BlockSpec((None, CONV_B - 1, WIDTH_B), lambda b, i: (b, 0, 0)),
            pl.BlockSpec((None, CONV_C - 1, WIDTH_C), lambda b, i: (b, 0, 0)),
        ],
        out_shape=[
            jax.ShapeDtypeStruct((rows, WIDTH_B), F32),
            jax.ShapeDtypeStruct((rows, WIDTH_C), F32),
            jax.ShapeDtypeStruct((batch, 1, WIDTH_B), F32),
            jax.ShapeDtypeStruct((batch, CONV_B - 1, WIDTH_B), F32),
            jax.ShapeDtypeStruct((batch, CONV_C - 1, WIDTH_C), F32),
        ],
        scratch_shapes=[
            pltpu.VMEM((tt + SUBLANES, WIDTH_B), F32),
            pltpu.VMEM((tt + SUBLANES, WIDTH_C), F32),
            pltpu.VMEM((tt, WIDTH_B), F32),
            pltpu.VMEM((tt, WIDTH_B), F32),
            pltpu.VMEM((1, WIDTH_B), F32),
        ],
        compiler_params=_params(("arbitrary", "arbitrary")),
        name="mix_p",
    )(zmix, *weights)


def _mix_s_kernel(z_ref, h0_ref, lch_ref, sch_ref, cw_ref, cb_ref, wa_ref, ba_ref, wx_ref, bx_ref,
                  lam_ref, sw_ref, ob_alias, oc_alias, ob_ref, oc_ref, hs_ref, pc_ref, *, t_new):
    tt = z_ref.shape[0]
    o3 = 2 * WIDTH_B
    tpos = lax.broadcasted_iota(jnp.int32, (tt, WIDTH_B), 0) % t_new

    def back(x, hist, s):
        return jnp.where(tpos >= s, pltpu.roll(x, s, 0), pltpu.roll(hist, tt - (t_new - s), 0))

    u = z_ref[:, 0:WIDTH_B]
    lch = lch_ref[...]
    xc = cw_ref[0:1, :] * back(u, lch, 3)
    xc = xc + cw_ref[1:2, :] * back(u, lch, 2)
    xc = xc + cw_ref[2:3, :] * back(u, lch, 1)
    xc = xc + cw_ref[3:4, :] * u
    xc = xc + cb_ref[...]
    a, b = _lru_coeffs(xc, wa_ref, ba_ref, wx_ref, bx_ref, lam_ref)
    b = b + a * h0_ref[...]
    for s in (1, 2, 4):
        keep = tpos >= s
        a_sh = jnp.where(keep, pltpu.roll(a, s, 0), 1.0)
        b_sh = jnp.where(keep, pltpu.roll(b, s, 0), 0.0)
        b = a * b_sh + b
        a = a * a_sh
    hs_ref[...] = b
    ob_ref[...] = b * jax.nn.gelu(z_ref[:, WIDTH_B:2 * WIDTH_B])

    pc = z_ref[:, o3 + WIDTH_C:o3 + 2 * WIDTH_C] * z_ref[:, o3 + 2 * WIDTH_C:o3 + 3 * WIDTH_C]
    sch = sch_ref[...]
    uc = sw_ref[0:1, :] * back(pc, sch, 2)
    uc = uc + sw_ref[1:2, :] * back(pc, sch, 1)
    uc = uc + sw_ref[2:3, :] * pc
    pc_ref[...] = pc
    oc_ref[...] = z_ref[:, o3:o3 + WIDTH_C] * uc


def _mix_s(zmix, h0e, lch, sch, weights, ob, oc, layer, row0, n_seq, t_new):
    ns = SEQ_PER_MIX_STEP
    tt = ns * t_new
    blk0 = row0 // tt
    rows = n_seq * t_new
    row_spec = lambda w_: pl.BlockSpec((tt, w_), lambda i: (i, 0))
    all_spec = lambda w_: pl.BlockSpec((tt, w_), lambda i: (blk0 + i, 0))
    in_specs = [all_spec(W_MIX), row_spec(WIDTH_B), row_spec(WIDTH_B), row_spec(WIDTH_C)]
    in_specs += _mix_weight_specs(layer)
    in_specs += [pl.BlockSpec(memory_space=pl.ANY)] * 2
    return pl.pallas_call(
        functools.partial(_mix_s_kernel, t_new=t_new),
        grid=(n_seq // ns,),
        in_specs=in_specs,
        out_specs=[all_spec(WIDTH_B), all_spec(WIDTH_C), row_spec(WIDTH_B), row_spec(WIDTH_C)],
        out_shape=[
            jax.ShapeDtypeStruct(ob.shape, ob.dtype),
            jax.ShapeDtypeStruct(oc.shape, oc.dtype),
            jax.ShapeDtypeStruct((rows, WIDTH_B), F32),
            jax.ShapeDtypeStruct((rows, WIDTH_C), F32),
        ],
        input_output_aliases={len(in_specs) - 2: 0, len(in_specs) - 1: 1},
        compiler_params=_params(("parallel",)),
        name="mix_s",
    )(zmix, h0e, lch, sch, *weights, ob, oc)


def _merge_kernel(x_ref, oa_ref, ob_ref, oc_ref, g_ref, wb_ref, wo_ref, o_ref):
    merged = None
    lo = 0
    for b, (o_b, width) in enumerate(((oa_ref, WIDTH_A), (ob_ref, WIDTH_B), (oc_ref, WIDTH_C))):
        proj = _dot(o_b[...].astype(BF16), wb_ref[lo:lo + width, :])
        term = g_ref[:, b * D_MODEL:(b + 1) * D_MODEL].astype(F32) * proj
        merged = term if merged is None else merged + term
        lo += width
    o_ref[...] = x_ref[...] + _dot(merged.astype(BF16), wo_ref[...])


def _merge(x, oa, ob, oc, g, wb, wo, layer):
    rows = x.shape[0]
    row_spec = lambda w_: pl.BlockSpec((ROW_TILE, w_), lambda i: (i, 0))
    return pl.pallas_call(
        _merge_kernel,
        grid=(rows // ROW_TILE,),
        in_specs=[
            row_spec(D_MODEL), row_spec(WIDTH_A), row_spec(WIDTH_B), row_spec(WIDTH_C), row_spec(W_G),
            _resident((None, WIDTH_A + WIDTH_B + WIDTH_C, D_MODEL), lambda i: (layer, 0, 0)),
            _resident((None, D_MODEL, D_MODEL), lambda i: (layer, 0, 0)),
        ],
        out_specs=row_spec(D_MODEL),
        out_shape=jax.ShapeDtypeStruct((rows, D_MODEL), F32),
        compiler_params=_params(("parallel",)),
        name="merge",
    )(x, oa, ob, oc, g, wb, wo)


def kernel(x_prompt, x_sample, cache_k, cache_v, cache_logf, state_lru_h, state_lru_conv, state_sc_conv,
           page_table, ffn1_norm, ffn1_w_in, ffn1_w_out, mix_norm, w_in, b_forget, q_norm, k_norm,
           lru_conv_w, lru_conv_b, lru_wa, lru_ba, lru_wx, lru_bx, lru_lambda, sc_conv_w, w_branch, w_out,
           ffn2_norm, ffn2_w_in, ffn2_w_out):
    batch, seq, _ = x_prompt.shape
    n_seq, t_new, _ = x_sample.shape
    depth = w_in.shape[0]
    rows_p = batch * seq
    rows_s = n_seq * t_new

    vec = lambda a: a[:, None, :]
    w1a, w2a = _to_bf16(ffn1_w_in), _to_bf16(ffn1_w_out)
    w1b, w2b = _to_bf16(ffn2_w_in), _to_bf16(ffn2_w_out)
    w_in_p = (w_in[:, :, :OFF_F].astype(BF16), w_in[:, :, OFF_MIX:].astype(BF16),
              jnp.pad(w_in[:, :, OFF_F:OFF_MIX], ((0, 0), (0, 0), (0, LANES - N_HEADS))).astype(BF16))
    bf_p = jnp.pad(b_forget, ((0, 0), (0, LANES - N_HEADS)))[:, None, :]
    qn_p = vec(jnp.tile(q_norm, (1, N_HEADS)))
    kn_p = vec(jnp.tile(k_norm, (1, N_HEADS)))
    eye = jnp.eye(N_BLOCKS_B, dtype=lru_wa.dtype)
    dense = lambda w: jnp.einsum('lgij,gh->lgihj', w, eye).reshape(depth, WIDTH_B, WIDTH_B).astype(BF16)
    mix_weights = (lru_conv_w, vec(lru_conv_b), dense(lru_wa), vec(lru_ba), dense(lru_wx), vec(lru_bx),
                   vec(lru_lambda), sc_conv_w)
    wb_b, wo_b = _to_bf16(w_branch), _to_bf16(w_out)
    ck4 = jnp.transpose(cache_k, (0, 1, 3, 4, 2))
    cv4 = jnp.transpose(cache_v, (0, 1, 3, 4, 2))
    clft = jnp.swapaxes(cache_logf, 2, 3)
    cpool = _pool_prefix(clft.reshape(-1, PAGE)).reshape(clft.shape)

    def seq_rows(state, at_end):
        r = state.shape[1]
        pad = (t_new - r, 0) if at_end else (0, t_new - r)
        return jnp.pad(state, ((0, 0), pad, (0, 0))).reshape(rows_s, state.shape[-1])

    per_seq = lambda a: a.reshape(n_seq, t_new, a.shape[-1])
    x = (x_prompt.reshape(rows_p, D_MODEL), x_sample.reshape(rows_s, D_MODEL))
    prev = None
    outs_p = [[] for _ in range(3)]
    outs_s = [[] for _ in range(6)]
    for l in range(depth):
        x = _ffn(x, vec(ffn1_norm), w1a, w2a, l, rows_p, rows_s, False)
        (qb, kb, vb, qs, ks, vs, lfs, kt, vt, lft, zmix, gates) = _inproj(
            x, vec(mix_norm), w_in_p, bf_p, qn_p, kn_p, l, depth, batch, seq, rows_s, prev)
        prev = (kt, vt, lft)
        ct = _cumsum(lft, l, batch, seq)
        oa = _attn_p(qb, kb, vb, ct.T, ct, batch, seq)
        oa = _attn_s(page_table, qs, ks, vs, lfs, ck4, cv4, cpool, oa, l, rows_p, t_new)
        ob, oc, h_p, lc_p, sc_p = _mix_p(zmix, mix_weights, l, batch, seq)
        ob, oc, hs_s, pc_s = _mix_s(zmix, seq_rows(state_lru_h[:, l][:, None, :], False),
                                    seq_rows(state_lru_conv[:, l], True),
                                    seq_rows(state_sc_conv[:, l], True), mix_weights, ob, oc,
                                    l, rows_p, n_seq, t_new)
        x = _merge(x, oa, ob, oc, gates, wb_b, wo_b, l)
        x = _ffn(x, vec(ffn2_norm), w1b, w2b, l, rows_p, rows_s, l == depth - 1)
        for dst, val in zip(outs_p, (h_p[:, 0], lc_p, sc_p)):
            dst.append(val)
        for dst, val in zip(outs_s, (ks.reshape(n_seq, t_new, N_HEADS, HEAD_DIM),
                                     vs.reshape(n_seq, t_new, N_HEADS, HEAD_DIM),
                                     per_seq(lfs[:, :N_HEADS]),
                                     per_seq(hs_s)[:, -1],
                                     per_seq(zmix[rows_p:, :WIDTH_B])[:, t_new - (CONV_B - 1):],
                                     per_seq(pc_s)[:, t_new - (CONV_C - 1):])):
            dst.append(val)
    stack = lambda xs: jnp.stack(xs, axis=1)
    kt, vt, lft = prev
    heads = lambda a: jnp.transpose(a.reshape(batch, depth, N_HEADS, HEAD_DIM, seq), (0, 1, 4, 2, 3))
    y_p, y_s = x
    return ((y_p.reshape(batch, seq, D_MODEL), y_s.reshape(n_seq, t_new, D_MODEL),
             heads(kt), heads(vt), jnp.swapaxes(lft, 2, 3))
            + tuple(stack(o) for o in outs_p) + tuple(stack(o) for o in outs_s))
```

```python
import functools
import math

import jax
import jax.numpy as jnp
from jax import lax
from jax.experimental import pallas as pl
from jax.experimental.pallas import tpu as pltpu

F32 = jnp.float32
BF16 = jnp.bfloat16

D_MODEL = 1024
N_HEADS = 8
HEAD_DIM = 64
WIDTH_A = N_HEADS * HEAD_DIM
WIDTH_B = 512
N_BLOCKS_B = 8
CONV_B = 4
LRU_C = 8.0
WIDTH_C = 512
CONV_C = 3
N_BRANCH = 3
FFN_HIDDEN = 2816
RMS_EPS = 1e-6
PAGE = 128

LANES = 128
SUBLANES = 8
VMEM_LIMIT = 56 * 1024 * 1024
LOG2E = math.log2(math.e)

OFF_F = 3 * WIDTH_A
OFF_MIX = OFF_F + N_HEADS
W_MIX = 2 * WIDTH_B + 3 * WIDTH_C
OFF_G = OFF_MIX + W_MIX
W_G = N_BRANCH * D_MODEL

NEG = -1e30
ROW_TILE = 512
MXU_TILE = 256
FFN_CHUNKS = ((0, 5 * MXU_TILE), (5 * MXU_TILE, FFN_HIDDEN))
ATT_TILE = 512
ATT_COLS = 4
ATT_KTILE = 256
MIX_TILE = 512
SEQ_PER_MIX_STEP = 64
POOL_TILE = 2560
CAST_BLOCK_BYTES = 6 * 1024 * 1024
PAGE_AHEAD = 2
PAGE_SLOTS = PAGE_AHEAD + 1


def _params(sem):
    return pltpu.CompilerParams(dimension_semantics=sem, vmem_limit_bytes=VMEM_LIMIT)


def _resident(shape, index_map):
    return pl.BlockSpec(shape, index_map, pipeline_mode=pl.Buffered(1))


def _rms(x, g):
    return (x * lax.rsqrt(jnp.mean(x * x, axis=-1, keepdims=True) + RMS_EPS)) * g


def _softplus(x):
    return jnp.maximum(x, 0.0) + jnp.log1p(jnp.exp(-jnp.abs(x)))


def _dot(a, b):
    return jnp.dot(a, b, preferred_element_type=F32)


def _dot_nt(a, b):
    return lax.dot_general(a, b, (((1,), (1,)), ((), ())), preferred_element_type=F32)


def _prompt_only(n_p):
    return lambda i: (jnp.minimum(i, n_p - 1), 0)


def _sample_only(n_p):
    return lambda i: (jnp.maximum(i - n_p, 0), 0)


def _cast_kernel(x_ref, o_ref):
    o_ref[...] = x_ref[...].astype(BF16)


def _to_bf16(w):
    depth, r, c = w.shape
    n_blk = 1
    while (r // n_blk) * c * 4 > CAST_BLOCK_BYTES or r % n_blk or (r // n_blk) % (2 * SUBLANES):
        n_blk += 1
    spec = pl.BlockSpec((None, r // n_blk, c), lambda d, i: (d, i, 0))
    return pl.pallas_call(
        _cast_kernel,
        grid=(depth, n_blk),
        in_specs=[spec],
        out_specs=spec,
        out_shape=jax.ShapeDtypeStruct(w.shape, BF16),
        compiler_params=_params(("parallel", "parallel")),
        name="to_bf16",
    )(w)


def _ffn_kernel(*refs, n_p, split_in, split_out):
    i = pl.program_id(0)
    n_x = 2 if split_in else 1
    g_ref, w1_ref, w2_ref = refs[n_x:n_x + 3]
    outs = refs[n_x + 3:]
    if split_in:
        x = jnp.where(i < n_p, refs[0][...], refs[1][...])
    else:
        x = refs[0][...]
    xn = _rms(x, g_ref[...]).astype(BF16)
    acc = jnp.zeros(x.shape, F32)
    for lo, hi in FFN_CHUNKS:
        gate = _dot(xn, w1_ref[:, lo:hi])
        up = _dot(xn, w1_ref[:, FFN_HIDDEN + lo:FFN_HIDDEN + hi])
        act = (gate * jax.nn.sigmoid(gate)) * up
        acc = acc + _dot(act.astype(BF16), w2_ref[lo:hi, :])
    y = x + 0.5 * acc
    if split_out:
        @pl.when(i < n_p)
        def _():
            outs[0][...] = y

        @pl.when(i >= n_p)
        def _():
            outs[1][...] = y
    else:
        outs[0][...] = y


def _ffn(xs, norm, w1, w2, layer, rows_p, rows_s, split_out):
    n_p = rows_p // ROW_TILE
    rows = rows_p + rows_s
    split_in = isinstance(xs, tuple)
    tile = (ROW_TILE, D_MODEL)
    pair = [pl.BlockSpec(tile, _prompt_only(n_p)), pl.BlockSpec(tile, _sample_only(n_p))]
    whole = [pl.BlockSpec(tile, lambda i: (i, 0))]
    if split_out:
        out_shape = [jax.ShapeDtypeStruct((rows_p, D_MODEL), F32), jax.ShapeDtypeStruct((rows_s, D_MODEL), F32)]
    else:
        out_shape = [jax.ShapeDtypeStruct((rows, D_MODEL), F32)]
    out = pl.pallas_call(
        functools.partial(_ffn_kernel, n_p=n_p, split_in=split_in, split_out=split_out),
        grid=(rows // ROW_TILE,),
        in_specs=(pair if split_in else whole) + [
            pl.BlockSpec((None, 1, D_MODEL), lambda i: (layer, 0, 0)),
            _resident((None, D_MODEL, 2 * FFN_HIDDEN), lambda i: (layer, 0, 0)),
            _resident((None, FFN_HIDDEN, D_MODEL), lambda i: (layer, 0, 0)),
        ],
        out_specs=pair if split_out else whole,
        out_shape=out_shape,
        compiler_params=_params(("arbitrary",)),
        name="ffn",
    )(*(xs if split_in else (xs,)), norm, w1, w2)
    return tuple(out) if split_out else out[0]


def _head_rms(x, gamma):
    m = x.shape[0]
    lane = lax.broadcasted_iota(jnp.int32, (m, LANES), 1)
    first = lane < HEAD_DIM
    outs = []
    for c in range(WIDTH_A // LANES):
        xc = x[:, c * LANES:(c + 1) * LANES]
        sq = xc * xc
        s_lo = jnp.sum(jnp.where(first, sq, 0.0), axis=-1, keepdims=True)
        s_hi = jnp.sum(jnp.where(first, 0.0, sq), axis=-1, keepdims=True)
        ms = jnp.where(first, s_lo, s_hi) * (1.0 / HEAD_DIM)
        outs.append((xc * lax.rsqrt(ms + RMS_EPS)) * gamma[:, c * LANES:(c + 1) * LANES])
    return jnp.concatenate(outs, axis=-1)


def _inproj_kernel(x_ref, g_ref, wa_ref, wm_ref, wf_ref, bf_ref, qn_ref, kn_ref, *rest, n_s, n_alias):
    (qb_ref, kb_ref, vb_ref, qs_ref, ks_ref, vs_ref, lfs_ref,
     kt_ref, vt_ref, lft_ref, mix_ref, gate_ref) = rest[n_alias:]
    i = pl.program_id(0)
    xn = _rms(x_ref[...], g_ref[...]).astype(BF16)
    qkv = _dot(xn, wa_ref[...])
    q = _head_rms(qkv[:, 0:WIDTH_A], qn_ref[...]) * (HEAD_DIM ** -0.5 * LOG2E)
    k = _head_rms(qkv[:, WIDTH_A:2 * WIDTH_A], kn_ref[...])
    v = qkv[:, 2 * WIDTH_A:3 * WIDTH_A]
    qb_ref[...] = q.astype(BF16)
    kb_ref[...] = k.astype(BF16)
    vb_ref[...] = v.astype(BF16)
    mix_ref[...] = _dot(xn, wm_ref[:, 0:W_MIX])
    gate_ref[...] = jax.nn.sigmoid(_dot(xn, wm_ref[:, W_MIX:W_MIX + W_G])).astype(BF16)
    f = _dot(xn, wf_ref[...]) + bf_ref[...]
    lf = jnp.minimum(f, 0.0) - jnp.log1p(jnp.exp(-jnp.abs(f)))

    kt_ref[...] = k.T
    vt_ref[...] = v.T
    lft_ref[...] = lf.T[0:N_HEADS, :]

    @pl.when(i < n_s)
    def _():
        qs_ref[...] = q
        ks_ref[...] = k
        vs_ref[...] = v
        lfs_ref[...] = lf


def _inproj(x, norm, w, bf, qn, kn, layer, depth, batch, seq, rows_s, prev):
    rows = x.shape[0]
    n_p = batch * seq // ROW_TILE
    n_s = rows_s // ROW_TILE
    per_seq = seq // ROW_TILE
    tile = lambda i: jnp.where(i < n_s, n_p + i, i - n_s)
    row_spec = lambda w_: pl.BlockSpec((ROW_TILE, w_), lambda i: (tile(i), 0))
    smp_spec = lambda w_: pl.BlockSpec((ROW_TILE, w_), lambda i: (jnp.minimum(i, n_s - 1), 0))
    vec_spec = lambda w_: pl.BlockSpec((None, 1, w_), lambda i: (layer, 0, 0))

    def state_spec(r):
        def im(i):
            j = jnp.maximum(i - n_s, 0)
            return (j // per_seq, layer, 0, j % per_seq)
        return pl.BlockSpec((None, None, r, ROW_TILE), im)

    n_alias = 0 if prev is None else 3
    w_spec = lambda w_: _resident((None, D_MODEL, w_), lambda i: (layer, 0, 0))
    in_specs = [row_spec(D_MODEL), vec_spec(D_MODEL), w_spec(3 * WIDTH_A), w_spec(W_MIX + W_G), w_spec(LANES),
                vec_spec(LANES), vec_spec(WIDTH_A), vec_spec(WIDTH_A)]
    n_in = len(in_specs)
    in_specs += [pl.BlockSpec(memory_space=pl.ANY)] * n_alias
    sds = jax.ShapeDtypeStruct
    return pl.pallas_call(
        functools.partial(_inproj_kernel, n_s=n_s, n_alias=n_alias),
        grid=(rows // ROW_TILE,),
        in_specs=in_specs,
        out_specs=[
            row_spec(WIDTH_A), row_spec(WIDTH_A), row_spec(WIDTH_A),
            smp_spec(WIDTH_A), smp_spec(WIDTH_A), smp_spec(WIDTH_A), smp_spec(LANES),
            state_spec(WIDTH_A), state_spec(WIDTH_A), state_spec(N_HEADS),
            row_spec(W_MIX), row_spec(W_G),
        ],
        out_shape=[
            sds((rows, WIDTH_A), BF16), sds((rows, WIDTH_A), BF16), sds((rows, WIDTH_A), BF16),
            sds((rows_s, WIDTH_A), F32), sds((rows_s, WIDTH_A), F32), sds((rows_s, WIDTH_A), F32),
            sds((rows_s, LANES), F32),
            sds((batch, depth, WIDTH_A, seq), F32), sds((batch, depth, WIDTH_A, seq), F32),
            sds((batch, depth, N_HEADS, seq), F32),
            sds((rows, W_MIX), F32), sds((rows, W_G), BF16),
        ],
        input_output_aliases={n_in + a: 7 + a for a in range(n_alias)},
        compiler_params=_params(("arbitrary",)),
        name="inproj",
    )(x, norm, *w, bf, qn, kn, *(prev or ()))


def _lane_prefix(x, n_valid):
    lane = lax.broadcasted_iota(jnp.int32, x.shape, 1)
    shift = 1
    while shift < n_valid:
        x = x + jnp.where(lane >= shift, pltpu.roll(x, shift, 1), 0.0)
        shift *= 2
    return x


def _cumsum_kernel(x_ref, o_ref):
    o_ref[...] = _lane_prefix(x_ref[...], x_ref.shape[1]) * LOG2E


def _cumsum(lft, layer, batch, seq):
    return pl.pallas_call(
        _cumsum_kernel,
        grid=(batch,),
        in_specs=[pl.BlockSpec((None, None, N_HEADS, seq), lambda b: (b, layer, 0, 0))],
        out_specs=pl.BlockSpec((N_HEADS, seq), lambda b: (0, b)),
        out_shape=jax.ShapeDtypeStruct((N_HEADS, batch * seq), F32),
        compiler_params=_params(("parallel",)),
        name="cumsum",
    )(lft)


def _pool_prefix_kernel(x_ref, o_ref):
    r = lax.broadcasted_iota(jnp.int32, (PAGE, PAGE), 0)
    c = lax.broadcasted_iota(jnp.int32, (PAGE, PAGE), 1)
    tri = jnp.where(r <= c, 1.0, 0.0).astype(F32)
    o_ref[...] = jnp.dot(x_ref[...], tri, precision=lax.Precision.HIGHEST, preferred_element_type=F32) * LOG2E


def _pool_prefix(pool):
    rows = pool.shape[0]
    return pl.pallas_call(
        _pool_prefix_kernel,
        grid=(rows // POOL_TILE,),
        in_specs=[pl.BlockSpec((POOL_TILE, PAGE), lambda i: (i, 0))],
        out_specs=pl.BlockSpec((POOL_TILE, PAGE), lambda i: (i, 0)),
        out_shape=jax.ShapeDtypeStruct(pool.shape, F32),
        compiler_params=_params(("parallel",)),
        name="pool_prefix",
    )(pool)


def _attn_p_kernel(q_ref, k_ref, v_ref, c_ref, ct_ref, o_ref, m_ref, acc_ref, cq_ref, vx_ref):
    t = ATT_TILE
    n_heads = 2 * ATT_COLS
    head0 = pl.program_id(1) * n_heads
    qi = pl.program_id(2)
    lane = lax.broadcasted_iota(jnp.int32, (t, LANES), 1)
    first = lane < HEAD_DIM
    cols = lambda c: slice(c * LANES, (c + 1) * LANES)

    @pl.when(qi == 0)
    def _():
        own = lax.broadcasted_iota(jnp.int32, (v_ref.shape[0], LANES), 1) < HEAD_DIM
        for c in range(ATT_COLS):
            v = v_ref[:, cols(c)]
            one = jnp.ones_like(v)
            vx_ref[2 * c] = jnp.where(own, v, one)
            vx_ref[2 * c + 1] = jnp.where(own, one, v)

    q_heads = []
    for c in range(ATT_COLS):
        q2 = q_ref[:, cols(c)]
        zero = jnp.zeros_like(q2)
        q_heads += [jnp.where(first, q2, zero), jnp.where(first, zero, q2)]
    crow = c_ref[...]
    hlane = lax.broadcasted_iota(jnp.int32, crow.shape, 1)
    for e in range(n_heads):
        cq = jnp.sum(jnp.where(hlane == head0 + e, crow, 0.0), axis=-1, keepdims=True)
        cq_ref[e] = jnp.broadcast_to(cq, (t, LANES))
    m_ref[...] = jnp.full(m_ref.shape, NEG, F32)
    acc_ref[...] = jnp.zeros(acc_ref.shape, F32)
    def block(r0, nr, start, width, row_shift):
        rs = slice(r0, r0 + nr)
        if row_shift is not None:
            row = lax.broadcasted_iota(jnp.int32, (nr, width), 0)
            col = lax.broadcasted_iota(jnp.int32, (nr, width), 1)
            visible = col <= row + row_shift
        for e in range(n_heads):
            kc = k_ref[pl.ds(start, width), cols(e // 2)]
            ck = ct_ref[pl.ds(head0 + e, 1), pl.ds(start, width)]
            u = _dot_nt(q_heads[e][rs], kc) - ck
            if row_shift is not None:
                u = jnp.where(visible, u, NEG)
            cq = cq_ref[e, rs]
            m_prev = m_ref[e, rs]
            m_new = jnp.maximum(m_prev, jnp.max(u, axis=-1, keepdims=True) + cq)
            alpha = jnp.exp2(m_prev - m_new)
            p = jnp.exp2(u + jnp.concatenate([cq - m_new] * (width // LANES), axis=1))
            acc_ref[e, rs] = alpha * acc_ref[e, rs] + _dot(p.astype(BF16), vx_ref[e, pl.ds(start, width), :])
            m_ref[e, rs] = m_new

    kt = ATT_KTILE

    def sweep(tile):
        for s in range(t // kt):
            block(0, t, pl.multiple_of(tile * t, t) + s * kt, kt, None)

    def body(j, carry):
        sweep(2 * j)
        sweep(2 * j + 1)
        return carry

    lax.fori_loop(0, lax.shift_right_logical(qi, 1), body, 0)

    @pl.when((qi & 1) == 1)
    def _():
        sweep(qi - 1)

    block(0, t, pl.multiple_of(qi * t, t), t, 0)
    for c in range(ATT_COLS):
        a0 = acc_ref[2 * c]
        a1 = acc_ref[2 * c + 1]
        o_ref[:, cols(c)] = jnp.where(first, a0 / pltpu.roll(a0, HEAD_DIM, 1), a1 / pltpu.roll(a1, HEAD_DIM, 1))


def _attn_p(qb, kb, vb, c_rows, ct, batch, seq):
    t = ATT_TILE
    nq = seq // t
    w = ATT_COLS * LANES
    return pl.pallas_call(
        _attn_p_kernel,
        grid=(batch, WIDTH_A // w, nq),
        in_specs=[
            pl.BlockSpec((t, w), lambda b, h, i: (b * nq + i, h)),
            pl.BlockSpec((seq, w), lambda b, h, i: (b, h)),
            pl.BlockSpec((seq, w), lambda b, h, i: (b, h)),
            pl.BlockSpec((t, N_HEADS), lambda b, h, i: (b * nq + i, 0)),
            pl.BlockSpec((N_HEADS, seq), lambda b, h, i: (0, b)),
        ],
        out_specs=pl.BlockSpec((t, w), lambda b, h, i: (b * nq + i, h)),
        out_shape=jax.ShapeDtypeStruct((qb.shape[0], WIDTH_A), F32),
        scratch_shapes=[
            pltpu.VMEM((2 * ATT_COLS, t, LANES), F32),
            pltpu.VMEM((2 * ATT_COLS, t, LANES), F32),
            pltpu.VMEM((2 * ATT_COLS, t, LANES), F32),
            pltpu.VMEM((2 * ATT_COLS, seq, LANES), BF16),
        ],
        compiler_params=_params(("parallel", "parallel", "arbitrary")),
        name="attn_p",
    )(qb, kb, vb, c_rows, ct)


def _transpose_rows8(x):
    pad = jnp.concatenate([x, jnp.zeros((LANES - SUBLANES, LANES), F32)], axis=0)
    return pad.T[0:SUBLANES, :]


def _attn_s_kernel(pt_ref, q_ref, kn_ref, vn_ref, lfn_ref, ck_hbm, cv_hbm, cc_hbm, o_alias, o_ref,
                   kbuf, vbuf, cbuf, sem, *, n_pages, t_new, layer, n_seq):
    n = pl.program_id(0)
    slot = lax.rem(n, PAGE_SLOTS)
    ahead = jnp.minimum(n + PAGE_AHEAD, n_seq - 1)
    slot_ahead = lax.rem(n + PAGE_AHEAD, PAGE_SLOTS)

    def page_copies(seq, slot_, j):
        pg = pt_ref[seq, j]
        return (pltpu.make_async_copy(ck_hbm.at[pg, layer], kbuf.at[slot_, j], sem.at[0, slot_]),
                pltpu.make_async_copy(cv_hbm.at[pg, layer], vbuf.at[slot_, j], sem.at[1, slot_]),
                pltpu.make_async_copy(cc_hbm.at[pg, layer], cbuf.at[slot_, j], sem.at[2, slot_]))

    @pl.when(n == 0)
    def _():
        for s in range(PAGE_AHEAD):
            for j in range(n_pages):
                for c in page_copies(min(s, n_seq - 1), s, j):
                    c.start()

    for j in range(n_pages):
        for c in page_copies(n, slot, j):
            c.wait()
    k_pages, v_pages, c_pages = [], [], []
    for j in range(n_pages):
        k_pages.append(kbuf[slot, j].reshape(WIDTH_A, PAGE).astype(BF16))
        v_pages.append(vbuf[slot, j].reshape(WIDTH_A, PAGE).astype(BF16))
        c_pages.append(cbuf[slot, j])
        for c in page_copies(ahead, slot_ahead, j):
            c.start()

    rows = N_HEADS * t_new
    rhead = lax.broadcasted_iota(jnp.int32, (rows, WIDTH_A), 0) // t_new
    chead = lax.broadcasted_iota(jnp.int32, (rows, WIDTH_A), 1) // HEAD_DIM
    diag = rhead == chead
    q = q_ref[...]
    qbd = jnp.where(diag, jnp.concatenate([q] * N_HEADS, axis=0), 0.0).astype(BF16)

    def rep(ct):
        return jnp.broadcast_to(ct[:, None, :], (N_HEADS, t_new, LANES)).reshape(rows, LANES)

    carry = jnp.zeros((N_HEADS, 1), F32)
    cts = []
    for within in c_pages:
        cts.append(within + carry)
        carry = carry + within[:, PAGE - 1:PAGE]
    lft_new = _transpose_rows8(lfn_ref[...])
    ct_new = _lane_prefix(lft_new, t_new) * LOG2E + carry
    lane = lax.broadcasted_iota(jnp.int32, (rows, LANES), 1)
    tq = lax.broadcasted_iota(jnp.int32, (rows, LANES), 0) % t_new
    ct_new_rep = rep(ct_new)
    cq = jnp.sum(jnp.where(lane == tq, ct_new_rep, 0.0), axis=-1, keepdims=True)

    n_past = n_pages * PAGE
    k_past = jnp.concatenate(k_pages, axis=1)
    v_past = jnp.concatenate(v_pages, axis=1)
    ct_past = jnp.concatenate(cts, axis=1)
    ct_past_rep = jnp.broadcast_to(ct_past[:, None, :], (N_HEADS, t_new, n_past)).reshape(rows, n_past)
    u_past = _dot(qbd, k_past) - ct_past_rep
    zeros_pad = jnp.zeros((PAGE - t_new, WIDTH_A), F32)
    k_new = jnp.concatenate([kn_ref[...], zeros_pad], axis=0).astype(BF16)
    v_new = jnp.concatenate([vn_ref[...], zeros_pad], axis=0).astype(BF16)
    u_new = jnp.where(lane <= tq, _dot_nt(qbd, k_new) - ct_new_rep, NEG)

    m = jnp.maximum(jnp.max(u_past, axis=-1, keepdims=True), jnp.max(u_new, axis=-1, keepdims=True)) + cq
    p_past = jnp.exp2((u_past + cq) - m)
    p_new = jnp.exp2((u_new + cq) - m)
    l = jnp.sum(p_past, axis=-1, keepdims=True) + jnp.sum(p_new, axis=-1, keepdims=True)
    acc = _dot_nt(p_past.astype(BF16), v_past) + _dot(p_new.astype(BF16), v_new)
    o = jnp.where(diag, acc / l, 0.0)
    out = o[0:t_new, :]
    for h in range(1, N_HEADS):
        out = out + o[h * t_new:(h + 1) * t_new, :]
    o_ref[...] = out

    @pl.when(n == n_seq - 1)
    def _():
        for s in range(1, PAGE_AHEAD + 1):
            for j in range(n_pages):
                for c in page_copies(n, lax.rem(n + s, PAGE_SLOTS), j):
                    c.wait()


def _attn_s(page_table, q, k, v, lf, cache_k, cache_v, cache_c, oa, layer, row0, t_new):
    n_seq, n_pages = page_table.shape
    blk0 = row0 // t_new
    new_spec = lambda w_: pl.BlockSpec((t_new, w_), lambda n, pt: (n, 0))
    in_specs = [new_spec(WIDTH_A), new_spec(WIDTH_A), new_spec(WIDTH_A), new_spec(LANES)]
    in_specs += [pl.BlockSpec(memory_space=pl.ANY)] * 4
    grid_spec = pltpu.PrefetchScalarGridSpec(
        num_scalar_prefetch=1,
        grid=(n_seq,),
        in_specs=in_specs,
        out_specs=pl.BlockSpec((t_new, WIDTH_A), lambda n, pt: (blk0 + n, 0)),
        scratch_shapes=[
            pltpu.VMEM((PAGE_SLOTS, n_pages, N_HEADS, HEAD_DIM, PAGE), F32),
            pltpu.VMEM((PAGE_SLOTS, n_pages, N_HEADS, HEAD_DIM, PAGE), F32),
            pltpu.VMEM((PAGE_SLOTS, n_pages, N_HEADS, PAGE), F32),
            pltpu.SemaphoreType.DMA((3, PAGE_SLOTS)),
        ],
    )
    return pl.pallas_call(
        functools.partial(_attn_s_kernel, n_pages=n_pages, t_new=t_new, layer=layer, n_seq=n_seq),
        grid_spec=grid_spec,
        out_shape=jax.ShapeDtypeStruct(oa.shape, oa.dtype),
        input_output_aliases={len(in_specs): 0},
        compiler_params=_params(("arbitrary",)),
        name="attn_s",
    )(page_table, q, k, v, lf, cache_k, cache_v, cache_c, oa)


def _lru_coeffs(xc, wa_ref, ba_ref, wx_ref, bx_ref, lam_ref):
    xb = xc.astype(BF16)
    sigmoid = lambda x: 0.5 * jnp.tanh(0.5 * x) + 0.5
    r = sigmoid(_dot(xb, wa_ref[...]) + ba_ref[...])
    ig = sigmoid(_dot(xb, wx_ref[...]) + bx_ref[...])
    log_a = (-LRU_C * r) * _softplus(-lam_ref[...])
    a = jnp.exp(log_a)
    y = 1.0 - a * a
    b = jnp.where(y > 0.0, y * lax.rsqrt(y), 0.0) * (ig * xc)
    return a, b


def _mix_p_kernel(z_ref, cw_ref, cb_ref, wa_ref, ba_ref, wx_ref, bx_ref, lam_ref, sw_ref,
                  ob_ref, oc_ref, h_ref, lc_ref, scc_ref,
                  ubuf, pbuf, a_s, b_s, hcar):
    tt = MIX_TILE
    hist = SUBLANES
    i = pl.program_id(1)

    @pl.when(i == 0)
    def _():
        ubuf[0:hist, :] = jnp.zeros((hist, WIDTH_B), F32)
        pbuf[0:hist, :] = jnp.zeros((hist, WIDTH_C), F32)
        hcar[...] = jnp.zeros(hcar.shape, F32)

    u = z_ref[:, 0:WIDTH_B]
    ubuf[hist:hist + tt, :] = u
    xc = cw_ref[0:1, :] * ubuf[hist - 3:hist - 3 + tt, :]
    xc = xc + cw_ref[1:2, :] * ubuf[hist - 2:hist - 2 + tt, :]
    xc = xc + cw_ref[2:3, :] * ubuf[hist - 1:hist - 1 + tt, :]
    xc = xc + cw_ref[3:4, :] * u
    xc = xc + cb_ref[...]
    ubuf[0:hist, :] = ubuf[tt:tt + hist, :]
    lc_ref[...] = z_ref[tt - (CONV_B - 1):tt, 0:WIDTH_B]

    a, b = _lru_coeffs(xc, wa_ref, ba_ref, wx_ref, bx_ref, lam_ref)
    a_s[...] = a
    b_s[...] = b
    srow = lax.broadcasted_iota(jnp.int32, (SUBLANES, WIDTH_B), 0)

    def chunk(c, h):
        r0 = pl.multiple_of(c * SUBLANES, SUBLANES)
        aa = a_s[pl.ds(r0, SUBLANES), :]
        bb = b_s[pl.ds(r0, SUBLANES), :]
        for s in (1, 2, 4):
            keep = srow >= s
            a_sh = jnp.where(keep, pltpu.roll(aa, s, 0), 1.0)
            b_sh = jnp.where(keep, pltpu.roll(bb, s, 0), 0.0)
            bb = aa * b_sh + bb
            aa = aa * a_sh
        hc = aa * h + bb
        b_s[pl.ds(r0, SUBLANES), :] = hc
        return hc[SUBLANES - 1:SUBLANES, :]

    h_last = lax.fori_loop(0, tt // SUBLANES, chunk, hcar[...], unroll=4)
    hcar[...] = h_last
    h_ref[...] = h_last
    ob_ref[...] = (b_s[...] * jax.nn.gelu(z_ref[:, WIDTH_B:2 * WIDTH_B])).astype(BF16)

    o3 = 2 * WIDTH_B
    pc = z_ref[:, o3 + WIDTH_C:o3 + 2 * WIDTH_C] * z_ref[:, o3 + 2 * WIDTH_C:o3 + 3 * WIDTH_C]
    pbuf[hist:hist + tt, :] = pc
    uc = sw_ref[0:1, :] * pbuf[hist - 2:hist - 2 + tt, :]
    uc = uc + sw_ref[1:2, :] * pbuf[hist - 1:hist - 1 + tt, :]
    uc = uc + sw_ref[2:3, :] * pc
    pbuf[0:hist, :] = pbuf[tt:tt + hist, :]
    scc_ref[...] = pbuf[hist + tt - (CONV_C - 1):hist + tt, :]
    oc_ref[...] = (z_ref[:, o3:o3 + WIDTH_C] * uc).astype(BF16)


def _mix_weight_specs(layer):
    def im(*_):
        return (layer, 0, 0)

    return [
        pl.BlockSpec((None, CONV_B, WIDTH_B), im),
        pl.BlockSpec((None, 1, WIDTH_B), im),
        pl.BlockSpec((None, WIDTH_B, WIDTH_B), im),
        pl.BlockSpec((None, 1, WIDTH_B), im),
        pl.BlockSpec((None, WIDTH_B, WIDTH_B), im),
        pl.BlockSpec((None, 1, WIDTH_B), im),
        pl.BlockSpec((None, 1, WIDTH_B), im),
        pl.BlockSpec((None, CONV_C, WIDTH_C), im),
    ]


def _mix_p(zmix, weights, layer, batch, seq):
    tt = MIX_TILE
    nt = seq // tt
    rows = zmix.shape[0]
    return pl.pallas_call(
        _mix_p_kernel,
        grid=(batch, nt),
        in_specs=[pl.BlockSpec((tt, W_MIX), lambda b, i: (b * nt + i, 0))] + _mix_weight_specs(layer),
        out_specs=[
            pl.BlockSpec((tt, WIDTH_B), lambda b, i: (b * nt + i, 0)),
            pl.BlockSpec((tt, WIDTH_C), lambda b, i: (b * nt + i, 0)),
            pl.BlockSpec((None, 1, WIDTH_B), lambda b, i: (b, 0, 0)),
            pl.BlockSpec((None, CONV_B - 1, WIDTH_B), lambda b, i: (b, 0, 0)),
            pl.BlockSpec((None, CONV_C - 1, WIDTH_C), lambda b, i: (b, 0, 0)),
        ],
        out_shape=[
            jax.ShapeDtypeStruct((rows, WIDTH_B), BF16),
            jax.ShapeDtypeStruct((rows, WIDTH_C), BF16),
            jax.ShapeDtypeStruct((batch, 1, WIDTH_B), F32),
            jax.ShapeDtypeStruct((batch, CONV_B - 1, WIDTH_B), F32),
            jax.ShapeDtypeStruct((batch, CONV_C - 1, WIDTH_C), F32),
        ],
        scratch_shapes=[
            pltpu.VMEM((tt + SUBLANES, WIDTH_B), F32),
            pltpu.VMEM((tt + SUBLANES, WIDTH_C), F32),
            pltpu.VMEM((tt, WIDTH_B), F32),
            pltpu.VMEM((tt, WIDTH_B), F32),
            pltpu.VMEM((1, WIDTH_B), F32),
        ],
        compiler_params=_params(("arbitrary", "arbitrary")),
        name="mix_p",
    )(zmix, *weights)


def _mix_s_kernel(z_ref, h0_ref, lch_ref, sch_ref, cw_ref, cb_ref, wa_ref, ba_ref, wx_ref, bx_ref,
                  lam_ref, sw_ref, ob_alias, oc_alias, ob_ref, oc_ref, hs_ref, pc_ref, *, t_new):
    tt = z_ref.shape[0]
    o3 = 2 * WIDTH_B
    tpos = lax.broadcasted_iota(jnp.int32, (tt, WIDTH_B), 0) % t_new

    def back(x, hist, s):
        return jnp.where(tpos >= s, pltpu.roll(x, s, 0), pltpu.roll(hist, tt - (t_new - s), 0))

    u = z_ref[:, 0:WIDTH_B]
    lch = lch_ref[...]
    xc = cw_ref[0:1, :] * back(u, lch, 3)
    xc = xc + cw_ref[1:2, :] * back(u, lch, 2)
    xc = xc + cw_ref[2:3, :] * back(u, lch, 1)
    xc = xc + cw_ref[3:4, :] * u
    xc = xc + cb_ref[...]
    a, b = _lru_coeffs(xc, wa_ref, ba_ref, wx_ref, bx_ref, lam_ref)
    b = b + a * h0_ref[...]
    for s in (1, 2, 4):
        keep = tpos >= s
        a_sh = jnp.where(keep, pltpu.roll(a, s, 0), 1.0)
        b_sh = jnp.where(keep, pltpu.roll(b, s, 0), 0.0)
        b = a * b_sh + b
        a = a * a_sh
    hs_ref[...] = b
    ob_ref[...] = (b * jax.nn.gelu(z_ref[:, WIDTH_B:2 * WIDTH_B])).astype(BF16)

    pc = z_ref[:, o3 + WIDTH_C:o3 + 2 * WIDTH_C] * z_ref[:, o3 + 2 * WIDTH_C:o3 + 3 * WIDTH_C]
    sch = sch_ref[...]
    uc = sw_ref[0:1, :] * back(pc, sch, 2)
    uc = uc + sw_ref[1:2, :] * back(pc, sch, 1)
    uc = uc + sw_ref[2:3, :] * pc
    pc_ref[...] = pc
    oc_ref[...] = (z_ref[:, o3:o3 + WIDTH_C] * uc).astype(BF16)


def _mix_s(zmix, h0e, lch, sch, weights, ob, oc, layer, row0, n_seq, t_new):
    ns = SEQ_PER_MIX_STEP
    tt = ns * t_new
    blk0 = row0 // tt
    rows = n_seq * t_new
    row_spec = lambda w_: pl.BlockSpec((tt, w_), lambda i: (i, 0))
    all_spec = lambda w_: pl.BlockSpec((tt, w_), lambda i: (blk0 + i, 0))
    in_specs = [all_spec(W_MIX), row_spec(WIDTH_B), row_spec(WIDTH_B), row_spec(WIDTH_C)]
    in_specs += _mix_weight_specs(layer)
    in_specs += [pl.BlockSpec(memory_space=pl.ANY)] * 2
    return pl.pallas_call(
        functools.partial(_mix_s_kernel, t_new=t_new),
        grid=(n_seq // ns,),
        in_specs=in_specs,
        out_specs=[all_spec(WIDTH_B), all_spec(WIDTH_C), row_spec(WIDTH_B), row_spec(WIDTH_C)],
        out_shape=[
            jax.ShapeDtypeStruct(ob.shape, ob.dtype),
            jax.ShapeDtypeStruct(oc.shape, oc.dtype),
            jax.ShapeDtypeStruct((rows, WIDTH_B), F32),
            jax.ShapeDtypeStruct((rows, WIDTH_C), F32),
        ],
        input_output_aliases={len(in_specs) - 2: 0, len(in_specs) - 1: 1},
        compiler_params=_params(("parallel",)),
        name="mix_s",
    )(zmix, h0e, lch, sch, *weights, ob, oc)


def _merge_kernel(x_ref, oa_ref, ob_ref, oc_ref, g_ref, wb_ref, wo_ref, o_ref):
    merged = None
    lo = 0
    for b, (o_b, width) in enumerate(((oa_ref, WIDTH_A), (ob_ref, WIDTH_B), (oc_ref, WIDTH_C))):
        proj = _dot(o_b[...].astype(BF16), wb_ref[lo:lo + width, :])
        term = g_ref[:, b * D_MODEL:(b + 1) * D_MODEL].astype(F32) * proj
        merged = term if merged is None else merged + term
        lo += width
    o_ref[...] = x_ref[...] + _dot(merged.astype(BF16), wo_ref[...])


def _merge(x, oa, ob, oc, g, wb, wo, layer):
    rows = x.shape[0]
    row_spec = lambda w_: pl.BlockSpec((ROW_TILE, w_), lambda i: (i, 0))
    return pl.pallas_call(
        _merge_kernel,
        grid=(rows // ROW_TILE,),
        in_specs=[
            row_spec(D_MODEL), row_spec(WIDTH_A), row_spec(WIDTH_B), row_spec(WIDTH_C), row_spec(W_G),
            _resident((None, WIDTH_A + WIDTH_B + WIDTH_C, D_MODEL), lambda i: (layer, 0, 0)),
            _resident((None, D_MODEL, D_MODEL), lambda i: (layer, 0, 0)),
        ],
        out_specs=row_spec(D_MODEL),
        out_shape=jax.ShapeDtypeStruct((rows, D_MODEL), F32),
        compiler_params=_params(("parallel",)),
        name="merge",
    )(x, oa, ob, oc, g, wb, wo)


def kernel(x_prompt, x_sample, cache_k, cache_v, cache_logf, state_lru_h, state_lru_conv, state_sc_conv,
           page_table, ffn1_norm, ffn1_w_in, ffn1_w_out, mix_norm, w_in, b_forget, q_norm, k_norm,
           lru_conv_w, lru_conv_b, lru_wa, lru_ba, lru_wx, lru_bx, lru_lambda, sc_conv_w, w_branch, w_out,
           ffn2_norm, ffn2_w_in, ffn2_w_out):
    batch, seq, _ = x_prompt.shape
    n_seq, t_new, _ = x_sample.shape
    depth = w_in.shape[0]
    rows_p = batch * seq
    rows_s = n_seq * t_new

    vec = lambda a: a[:, None, :]
    w1a, w2a = _to_bf16(ffn1_w_in), _to_bf16(ffn1_w_out)
    w1b, w2b = _to_bf16(ffn2_w_in), _to_bf16(ffn2_w_out)
    w_in_p = (w_in[:, :, :OFF_F].astype(BF16), w_in[:, :, OFF_MIX:].astype(BF16),
              jnp.pad(w_in[:, :, OFF_F:OFF_MIX], ((0, 0), (0, 0), (0, LANES - N_HEADS))).astype(BF16))
    bf_p = jnp.pad(b_forget, ((0, 0), (0, LANES - N_HEADS)))[:, None, :]
    qn_p = vec(jnp.tile(q_norm, (1, N_HEADS)))
    kn_p = vec(jnp.tile(k_norm, (1, N_HEADS)))
    eye = jnp.eye(N_BLOCKS_B, dtype=lru_wa.dtype)
    dense = lambda w: jnp.einsum('lgij,gh->lgihj', w, eye).reshape(depth, WIDTH_B, WIDTH_B).astype(BF16)
    mix_weights = (lru_conv_w, vec(lru_conv_b), dense(lru_wa), vec(lru_ba), dense(lru_wx), vec(lru_bx),
                   vec(lru_lambda), sc_conv_w)
    wb_b, wo_b = _to_bf16(w_branch), _to_bf16(w_out)
    ck4 = jnp.transpose(cache_k, (0, 1, 3, 4, 2))
    cv4 = jnp.transpose(cache_v, (0, 1, 3, 4, 2))
    clft = jnp.swapaxes(cache_logf, 2, 3)
    cpool = _pool_prefix(clft.reshape(-1, PAGE)).reshape(clft.shape)

    def seq_rows(state, at_end):
        r = state.shape[1]
        pad = (t_new - r, 0) if at_end else (0, t_new - r)
        return jnp.pad(state, ((0, 0), pad, (0, 0))).reshape(rows_s, state.shape[-1])

    per_seq = lambda a: a.reshape(n_seq, t_new, a.shape[-1])
    x = (x_prompt.reshape(rows_p, D_MODEL), x_sample.reshape(rows_s, D_MODEL))
    prev = None
    outs_p = [[] for _ in range(3)]
    outs_s = [[] for _ in range(6)]
    for l in range(depth):
        x = _ffn(x, vec(ffn1_norm), w1a, w2a, l, rows_p, rows_s, False)
        (qb, kb, vb, qs, ks, vs, lfs, kt, vt, lft, zmix, gates) = _inproj(
            x, vec(mix_norm), w_in_p, bf_p, qn_p, kn_p, l, depth, batch, seq, rows_s, prev)
        prev = (kt, vt, lft)
        ct = _cumsum(lft, l, batch, seq)
        oa = _attn_p(qb, kb, vb, ct.T, ct, batch, seq)
        oa = _attn_s(page_table, qs, ks, vs, lfs, ck4, cv4, cpool, oa, l, rows_p, t_new)
        ob, oc, h_p, lc_p, sc_p = _mix_p(zmix, mix_weights, l, batch, seq)
        ob, oc, hs_s, pc_s = _mix_s(zmix, seq_rows(state_lru_h[:, l][:, None, :], False),
                                    seq_rows(state_lru_conv[:, l], True),
                                    seq_rows(state_sc_conv[:, l], True), mix_weights, ob, oc,
                                    l, rows_p, n_seq, t_new)
        x = _merge(x, oa, ob, oc, gates, wb_b, wo_b, l)
        x = _ffn(x, vec(ffn2_norm), w1b, w2b, l, rows_p, rows_s, l == depth - 1)
        for dst, val in zip(outs_p, (h_p[:, 0], lc_p, sc_p)):
            dst.append(val)
        for dst, val in zip(outs_s, (ks.reshape(n_seq, t_new, N_HEADS, HEAD_DIM),
                                     vs.reshape(n_seq, t_new, N_HEADS, HEAD_DIM),
                                     per_seq(lfs[:, :N_HEADS]),
                                     per_seq(hs_s)[:, -1],
                                     per_seq(zmix[rows_p:, :WIDTH_B])[:, t_new - (CONV_B - 1):],
                                     per_seq(pc_s)[:, t_new - (CONV_C - 1):])):
            dst.append(val)
    stack = lambda xs: jnp.stack(xs, axis=1)
    kt, vt, lft = prev
    heads = lambda a: jnp.transpose(a.reshape(batch, depth, N_HEADS, HEAD_DIM, seq), (0, 1, 4, 2, 3))
    y_p, y_s = x
    return ((y_p.reshape(batch, seq, D_MODEL), y_s.reshape(n_seq, t_new, D_MODEL),
             heads(kt), heads(vt), jnp.swapaxes(lft, 2, 3))
            + tuple(stack(o) for o in outs_p) + tuple(stack(o) for o in outs_s))
```

```python
import functools
import math

import jax
import jax.numpy as jnp
from jax import lax
from jax.experimental import pallas as pl
from jax.experimental.pallas import tpu as pltpu

F32 = jnp.float32
BF16 = jnp.bfloat16

D_MODEL = 1024
N_HEADS = 8
HEAD_DIM = 64
WIDTH_A = N_HEADS * HEAD_DIM
WIDTH_B = 512
N_BLOCKS_B = 8
CONV_B = 4
LRU_C = 8.0
WIDTH_C = 512
CONV_C = 3
N_BRANCH = 3
FFN_HIDDEN = 2816
RMS_EPS = 1e-6
PAGE = 128

LANES = 128
SUBLANES = 8
VMEM_LIMIT = 56 * 1024 * 1024
LOG2E = math.log2(math.e)

OFF_F = 3 * WIDTH_A
OFF_MIX = OFF_F + N_HEADS
W_MIX = 2 * WIDTH_B + 3 * WIDTH_C
OFF_G = OFF_MIX + W_MIX
W_G = N_BRANCH * D_MODEL

NEG = -1e30
ROW_TILE = 512
MXU_TILE = 256
FFN_CHUNKS = ((0, 4 * MXU_TILE), (4 * MXU_TILE, 8 * MXU_TILE), (8 * MXU_TILE, FFN_HIDDEN))
ATT_TILE = 512
ATT_COLS = 4
ATT_KTILE = 256
MIX_TILE = 512
SEQ_PER_MIX_STEP = 64
POOL_TILE = 2560
CAST_BLOCK_BYTES = 6 * 1024 * 1024
PAGE_AHEAD = 2
PAGE_SLOTS = PAGE_AHEAD + 1


def _params(sem):
    return pltpu.CompilerParams(dimension_semantics=sem, vmem_limit_bytes=VMEM_LIMIT)


def _resident(shape, index_map):
    return pl.BlockSpec(shape, index_map, pipeline_mode=pl.Buffered(1))


def _rms(x, g):
    return (x * lax.rsqrt(jnp.mean(x * x, axis=-1, keepdims=True) + RMS_EPS)) * g


def _softplus(x):
    return jnp.maximum(x, 0.0) + jnp.log1p(jnp.exp(-jnp.abs(x)))


def _dot(a, b):
    return jnp.dot(a, b, preferred_element_type=F32)


def _dot_nt(a, b):
    return lax.dot_general(a, b, (((1,), (1,)), ((), ())), preferred_element_type=F32)


def _prompt_only(n_p):
    return lambda i: (jnp.minimum(i, n_p - 1), 0)


def _sample_only(n_p):
    return lambda i: (jnp.maximum(i - n_p, 0), 0)


def _cast_kernel(x_ref, o_ref):
    o_ref[...] = x_ref[...].astype(BF16)


def _to_bf16(w):
    depth, r, c = w.shape
    n_blk = 1
    while (r // n_blk) * c * 4 > CAST_BLOCK_BYTES or r % n_blk or (r // n_blk) % (2 * SUBLANES):
        n_blk += 1
    spec = pl.BlockSpec((None, r // n_blk, c), lambda d, i: (d, i, 0))
    return pl.pallas_call(
        _cast_kernel,
        grid=(depth, n_blk),
        in_specs=[spec],
        out_specs=spec,
        out_shape=jax.ShapeDtypeStruct(w.shape, BF16),
        compiler_params=_params(("parallel", "parallel")),
        name="to_bf16",
    )(w)


def _ffn_kernel(*refs, n_p, split_in, split_out):
    i = pl.program_id(0)
    n_x = 2 if split_in else 1
    g_ref, w1_ref, w2_ref = refs[n_x:n_x + 3]
    outs = refs[n_x + 3:]
    if split_in:
        x = jnp.where(i < n_p, refs[0][...], refs[1][...])
    else:
        x = refs[0][...]
    xn = _rms(x, g_ref[...]).astype(BF16)
    acc = jnp.zeros(x.shape, F32)
    for lo, hi in FFN_CHUNKS:
        gate = _dot(xn, w1_ref[:, lo:hi])
        up = _dot(xn, w1_ref[:, FFN_HIDDEN + lo:FFN_HIDDEN + hi])
        act = (gate * jax.nn.sigmoid(gate)) * up
        acc = acc + _dot(act.astype(BF16), w2_ref[lo:hi, :])
    y = x + 0.5 * acc
    if split_out:
        @pl.when(i < n_p)
        def _():
            outs[0][...] = y

        @pl.when(i >= n_p)
        def _():
            outs[1][...] = y
    else:
        outs[0][...] = y


def _ffn(xs, norm, w1, w2, layer, rows_p, rows_s, split_out):
    n_p = rows_p // ROW_TILE
    rows = rows_p + rows_s
    split_in = isinstance(xs, tuple)
    tile = (ROW_TILE, D_MODEL)
    pair = [pl.BlockSpec(tile, _prompt_only(n_p)), pl.BlockSpec(tile, _sample_only(n_p))]
    whole = [pl.BlockSpec(tile, lambda i: (i, 0))]
    if split_out:
        out_shape = [jax.ShapeDtypeStruct((rows_p, D_MODEL), F32), jax.ShapeDtypeStruct((rows_s, D_MODEL), F32)]
    else:
        out_shape = [jax.ShapeDtypeStruct((rows, D_MODEL), F32)]
    out = pl.pallas_call(
        functools.partial(_ffn_kernel, n_p=n_p, split_in=split_in, split_out=split_out),
        grid=(rows // ROW_TILE,),
        in_specs=(pair if split_in else whole) + [
            pl.BlockSpec((None, 1, D_MODEL), lambda i: (layer, 0, 0)),
            _resident((None, D_MODEL, 2 * FFN_HIDDEN), lambda i: (layer, 0, 0)),
            _resident((None, FFN_HIDDEN, D_MODEL), lambda i: (layer, 0, 0)),
        ],
        out_specs=pair if split_out else whole,
        out_shape=out_shape,
        compiler_params=_params(("arbitrary",)),
        name="ffn",
    )(*(xs if split_in else (xs,)), norm, w1, w2)
    return tuple(out) if split_out else out[0]


def _head_rms(x, gamma):
    m = x.shape[0]
    lane = lax.broadcasted_iota(jnp.int32, (m, LANES), 1)
    first = lane < HEAD_DIM
    outs = []
    for c in range(WIDTH_A // LANES):
        xc = x[:, c * LANES:(c + 1) * LANES]
        sq = xc * xc
        s_lo = jnp.sum(jnp.where(first, sq, 0.0), axis=-1, keepdims=True)
        s_hi = jnp.sum(jnp.where(first, 0.0, sq), axis=-1, keepdims=True)
        ms = jnp.where(first, s_lo, s_hi) * (1.0 / HEAD_DIM)
        outs.append((xc * lax.rsqrt(ms + RMS_EPS)) * gamma[:, c * LANES:(c + 1) * LANES])
    return jnp.concatenate(outs, axis=-1)


def _inproj_kernel(x_ref, g_ref, wa_ref, wm_ref, wf_ref, bf_ref, qn_ref, kn_ref, *rest, n_s, n_alias):
    (qb_ref, kb_ref, vb_ref, qs_ref, ks_ref, vs_ref, lfs_ref,
     kt_ref, vt_ref, lft_ref, mix_ref, gate_ref) = rest[n_alias:]
    i = pl.program_id(0)
    xn = _rms(x_ref[...], g_ref[...]).astype(BF16)
    qkv = _dot(xn, wa_ref[...])
    q = _head_rms(qkv[:, 0:WIDTH_A], qn_ref[...]) * (HEAD_DIM ** -0.5 * LOG2E)
    k = _head_rms(qkv[:, WIDTH_A:2 * WIDTH_A], kn_ref[...])
    v = qkv[:, 2 * WIDTH_A:3 * WIDTH_A]
    qb_ref[...] = q.astype(BF16)
    kb_ref[...] = k.astype(BF16)
    vb_ref[...] = v.astype(BF16)
    mix_ref[...] = _dot(xn, wm_ref[:, 0:W_MIX])
    gate_ref[...] = jax.nn.sigmoid(_dot(xn, wm_ref[:, W_MIX:W_MIX + W_G])).astype(BF16)
    f = _dot(xn, wf_ref[...]) + bf_ref[...]
    lf = jnp.minimum(f, 0.0) - jnp.log1p(jnp.exp(-jnp.abs(f)))

    kt_ref[...] = k.T
    vt_ref[...] = v.T
    lft_ref[...] = lf.T[0:N_HEADS, :]

    @pl.when(i < n_s)
    def _():
        qs_ref[...] = q
        ks_ref[...] = k
        vs_ref[...] = v
        lfs_ref[...] = lf


def _inproj(x, norm, w, bf, qn, kn, layer, depth, batch, seq, rows_s, prev):
    rows = x.shape[0]
    n_p = batch * seq // ROW_TILE
    n_s = rows_s // ROW_TILE
    per_seq = seq // ROW_TILE
    tile = lambda i: jnp.where(i < n_s, n_p + i, i - n_s)
    row_spec = lambda w_: pl.BlockSpec((ROW_TILE, w_), lambda i: (tile(i), 0))
    smp_spec = lambda w_: pl.BlockSpec((ROW_TILE, w_), lambda i: (jnp.minimum(i, n_s - 1), 0))
    vec_spec = lambda w_: pl.BlockSpec((None, 1, w_), lambda i: (layer, 0, 0))

    def state_spec(r):
        def im(i):
            j = jnp.maximum(i - n_s, 0)
            return (j // per_seq, layer, 0, j % per_seq)
        return pl.BlockSpec((None, None, r, ROW_TILE), im)

    n_alias = 0 if prev is None else 3
    w_spec = lambda w_: _resident((None, D_MODEL, w_), lambda i: (layer, 0, 0))
    in_specs = [row_spec(D_MODEL), vec_spec(D_MODEL), w_spec(3 * WIDTH_A), w_spec(W_MIX + W_G), w_spec(LANES),
                vec_spec(LANES), vec_spec(WIDTH_A), vec_spec(WIDTH_A)]
    n_in = len(in_specs)
    in_specs += [pl.BlockSpec(memory_space=pl.ANY)] * n_alias
    sds = jax.ShapeDtypeStruct
    return pl.pallas_call(
        functools.partial(_inproj_kernel, n_s=n_s, n_alias=n_alias),
        grid=(rows // ROW_TILE,),
        in_specs=in_specs,
        out_specs=[
            row_spec(WIDTH_A), row_spec(WIDTH_A), row_spec(WIDTH_A),
            smp_spec(WIDTH_A), smp_spec(WIDTH_A), smp_spec(WIDTH_A), smp_spec(LANES),
            state_spec(WIDTH_A), state_spec(WIDTH_A), state_spec(N_HEADS),
            row_spec(W_MIX), row_spec(W_G),
        ],
        out_shape=[
            sds((rows, WIDTH_A), BF16), sds((rows, WIDTH_A), BF16), sds((rows, WIDTH_A), BF16),
            sds((rows_s, WIDTH_A), F32), sds((rows_s, WIDTH_A), F32), sds((rows_s, WIDTH_A), F32),
            sds((rows_s, LANES), F32),
            sds((batch, depth, WIDTH_A, seq), F32), sds((batch, depth, WIDTH_A, seq), F32),
            sds((batch, depth, N_HEADS, seq), F32),
            sds((rows, W_MIX), F32), sds((rows, W_G), BF16),
        ],
        input_output_aliases={n_in + a: 7 + a for a in range(n_alias)},
        compiler_params=_params(("arbitrary",)),
        name="inproj",
    )(x, norm, *w, bf, qn, kn, *(prev or ()))


def _lane_prefix(x, n_valid):
    lane = lax.broadcasted_iota(jnp.int32, x.shape, 1)
    shift = 1
    while shift < n_valid:
        x = x + jnp.where(lane >= shift, pltpu.roll(x, shift, 1), 0.0)
        shift *= 2
    return x


def _cumsum_kernel(x_ref, o_ref):
    o_ref[...] = _lane_prefix(x_ref[...], x_ref.shape[1]) * LOG2E


def _cumsum(lft, layer, batch, seq):
    return pl.pallas_call(
        _cumsum_kernel,
        grid=(batch,),
        in_specs=[pl.BlockSpec((None, None, N_HEADS, seq), lambda b: (b, layer, 0, 0))],
        out_specs=pl.BlockSpec((N_HEADS, seq), lambda b: (0, b)),
        out_shape=jax.ShapeDtypeStruct((N_HEADS, batch * seq), F32),
        compiler_params=_params(("parallel",)),
        name="cumsum",
    )(lft)


def _pool_prefix_kernel(x_ref, o_ref):
    r = lax.broadcasted_iota(jnp.int32, (PAGE, PAGE), 0)
    c = lax.broadcasted_iota(jnp.int32, (PAGE, PAGE), 1)
    tri = jnp.where(r <= c, 1.0, 0.0).astype(F32)
    o_ref[...] = jnp.dot(x_ref[...], tri, precision=lax.Precision.HIGHEST, preferred_element_type=F32) * LOG2E


def _pool_prefix(pool):
    rows = pool.shape[0]
    return pl.pallas_call(
        _pool_prefix_kernel,
        grid=(rows // POOL_TILE,),
        in_specs=[pl.BlockSpec((POOL_TILE, PAGE), lambda i: (i, 0))],
        out_specs=pl.BlockSpec((POOL_TILE, PAGE), lambda i: (i, 0)),
        out_shape=jax.ShapeDtypeStruct(pool.shape, F32),
        compiler_params=_params(("parallel",)),
        name="pool_prefix",
    )(pool)


def _attn_p_kernel(q_ref, k_ref, v_ref, c_ref, ct_ref, o_ref, m_ref, acc_ref, cq_ref, vx_ref):
    t = ATT_TILE
    n_heads = 2 * ATT_COLS
    head0 = pl.program_id(1) * n_heads
    qi = pl.program_id(2)
    lane = lax.broadcasted_iota(jnp.int32, (t, LANES), 1)
    first = lane < HEAD_DIM
    cols = lambda c: slice(c * LANES, (c + 1) * LANES)

    @pl.when(qi == 0)
    def _():
        own = lax.broadcasted_iota(jnp.int32, (v_ref.shape[0], LANES), 1) < HEAD_DIM
        for c in range(ATT_COLS):
            v = v_ref[:, cols(c)]
            one = jnp.ones_like(v)
            vx_ref[2 * c] = jnp.where(own, v, one)
            vx_ref[2 * c + 1] = jnp.where(own, one, v)

    q_heads = []
    for c in range(ATT_COLS):
        q2 = q_ref[:, cols(c)]
        zero = jnp.zeros_like(q2)
        q_heads += [jnp.where(first, q2, zero), jnp.where(first, zero, q2)]
    crow = c_ref[...]
    hlane = lax.broadcasted_iota(jnp.int32, crow.shape, 1)
    for e in range(n_heads):
        cq = jnp.sum(jnp.where(hlane == head0 + e, crow, 0.0), axis=-1, keepdims=True)
        cq_ref[e] = jnp.broadcast_to(cq, (t, LANES))
    m_ref[...] = jnp.full(m_ref.shape, NEG, F32)
    acc_ref[...] = jnp.zeros(acc_ref.shape, F32)
    def block(r0, nr, start, width, row_shift):
        rs = slice(r0, r0 + nr)
        if row_shift is not None:
            row = lax.broadcasted_iota(jnp.int32, (nr, width), 0)
            col = lax.broadcasted_iota(jnp.int32, (nr, width), 1)
            visible = col <= row + row_shift
        for e in range(n_heads):
            kc = k_ref[pl.ds(start, width), cols(e // 2)]
            ck = ct_ref[pl.ds(head0 + e, 1), pl.ds(start, width)]
            u = _dot_nt(q_heads[e][rs], kc) - ck
            if row_shift is not None:
                u = jnp.where(visible, u, NEG)
            cq = cq_ref[e, rs]
            m_prev = m_ref[e, rs]
            m_new = jnp.maximum(m_prev, jnp.max(u, axis=-1, keepdims=True) + cq)
            alpha = jnp.exp2(m_prev - m_new)
            p = jnp.exp2(u + jnp.concatenate([cq - m_new] * (width // LANES), axis=1))
            acc_ref[e, rs] = alpha * acc_ref[e, rs] + _dot(p.astype(BF16), vx_ref[e, pl.ds(start, width), :])
            m_ref[e, rs] = m_new

    kt = ATT_KTILE

    def sweep(tile):
        for s in range(t // kt):
            block(0, t, pl.multiple_of(tile * t, t) + s * kt, kt, None)

    def body(j, carry):
        sweep(2 * j)
        sweep(2 * j + 1)
        return carry

    lax.fori_loop(0, lax.shift_right_logical(qi, 1), body, 0)

    @pl.when((qi & 1) == 1)
    def _():
        sweep(qi - 1)

    block(0, t, pl.multiple_of(qi * t, t), t, 0)
    for c in range(ATT_COLS):
        a0 = acc_ref[2 * c]
        a1 = acc_ref[2 * c + 1]
        o_ref[:, cols(c)] = jnp.where(first, a0 / pltpu.roll(a0, HEAD_DIM, 1), a1 / pltpu.roll(a1, HEAD_DIM, 1))


def _attn_p(qb, kb, vb, c_rows, ct, batch, seq):
    t = ATT_TILE
    nq = seq // t
    w = ATT_COLS * LANES
    return pl.pallas_call(
        _attn_p_kernel,
        grid=(batch, WIDTH_A // w, nq),
        in_specs=[
            pl.BlockSpec((t, w), lambda b, h, i: (b * nq + i, h)),
            pl.BlockSpec((seq, w), lambda b, h, i: (b, h)),
            pl.BlockSpec((seq, w), lambda b, h, i: (b, h)),
            pl.BlockSpec((t, N_HEADS), lambda b, h, i: (b * nq + i, 0)),
            pl.BlockSpec((N_HEADS, seq), lambda b, h, i: (0, b)),
        ],
        out_specs=pl.BlockSpec((t, w), lambda b, h, i: (b * nq + i, h)),
        out_shape=jax.ShapeDtypeStruct((qb.shape[0], WIDTH_A), F32),
        scratch_shapes=[
            pltpu.VMEM((2 * ATT_COLS, t, LANES), F32),
            pltpu.VMEM((2 * ATT_COLS, t, LANES), F32),
            pltpu.VMEM((2 * ATT_COLS, t, LANES), F32),
            pltpu.VMEM((2 * ATT_COLS, seq, LANES), BF16),
        ],
        compiler_params=_params(("parallel", "parallel", "arbitrary")),
        name="attn_p",
    )(qb, kb, vb, c_rows, ct)


def _transpose_rows8(x):
    pad = jnp.concatenate([x, jnp.zeros((LANES - SUBLANES, LANES), F32)], axis=0)
    return pad.T[0:SUBLANES, :]


def _attn_s_kernel(pt_ref, q_ref, kn_ref, vn_ref, lfn_ref, ck_hbm, cv_hbm, cc_hbm, o_alias, o_ref,
                   kbuf, vbuf, cbuf, sem, *, n_pages, t_new, layer, n_seq):
    n = pl.program_id(0)
    slot = lax.rem(n, PAGE_SLOTS)
    ahead = jnp.minimum(n + PAGE_AHEAD, n_seq - 1)
    slot_ahead = lax.rem(n + PAGE_AHEAD, PAGE_SLOTS)

    def page_copies(seq, slot_, j):
        pg = pt_ref[seq, j]
        return (pltpu.make_async_copy(ck_hbm.at[pg, layer], kbuf.at[slot_, j], sem.at[0, slot_]),
                pltpu.make_async_copy(cv_hbm.at[pg, layer], vbuf.at[slot_, j], sem.at[1, slot_]),
                pltpu.make_async_copy(cc_hbm.at[pg, layer], cbuf.at[slot_, j], sem.at[2, slot_]))

    @pl.when(n == 0)
    def _():
        for s in range(PAGE_AHEAD):
            for j in range(n_pages):
                for c in page_copies(min(s, n_seq - 1), s, j):
                    c.start()

    for j in range(n_pages):
        for c in page_copies(n, slot, j):
            c.wait()
    k_pages, v_pages, c_pages = [], [], []
    for j in range(n_pages):
        k_pages.append(kbuf[slot, j].reshape(WIDTH_A, PAGE).astype(BF16))
        v_pages.append(vbuf[slot, j].reshape(WIDTH_A, PAGE).astype(BF16))
        c_pages.append(cbuf[slot, j])
        for c in page_copies(ahead, slot_ahead, j):
            c.start()

    rows = N_HEADS * t_new
    rhead = lax.broadcasted_iota(jnp.int32, (rows, WIDTH_A), 0) // t_new
    chead = lax.broadcasted_iota(jnp.int32, (rows, WIDTH_A), 1) // HEAD_DIM
    diag = rhead == chead
    q = q_ref[...]
    qbd = jnp.where(diag, jnp.concatenate([q] * N_HEADS, axis=0), 0.0).astype(BF16)

    def rep(ct):
        return jnp.broadcast_to(ct[:, None, :], (N_HEADS, t_new, LANES)).reshape(rows, LANES)

    carry = jnp.zeros((N_HEADS, 1), F32)
    cts = []
    for within in c_pages:
        cts.append(within + carry)
        carry = carry + within[:, PAGE - 1:PAGE]
    lft_new = _transpose_rows8(lfn_ref[...])
    ct_new = _lane_prefix(lft_new, t_new) * LOG2E + carry
    lane = lax.broadcasted_iota(jnp.int32, (rows, LANES), 1)
    tq = lax.broadcasted_iota(jnp.int32, (rows, LANES), 0) % t_new
    ct_new_rep = rep(ct_new)
    cq = jnp.sum(jnp.where(lane == tq, ct_new_rep, 0.0), axis=-1, keepdims=True)

    n_past = n_pages * PAGE
    k_past = jnp.concatenate(k_pages, axis=1)
    v_past = jnp.concatenate(v_pages, axis=1)
    ct_past = jnp.concatenate(cts, axis=1)
    ct_past_rep = jnp.broadcast_to(ct_past[:, None, :], (N_HEADS, t_new, n_past)).reshape(rows, n_past)
    u_past = _dot(qbd, k_past) - ct_past_rep
    zeros_pad = jnp.zeros((PAGE - t_new, WIDTH_A), F32)
    k_new = jnp.concatenate([kn_ref[...], zeros_pad], axis=0).astype(BF16)
    v_new = jnp.concatenate([vn_ref[...], zeros_pad], axis=0).astype(BF16)
    u_new = jnp.where(lane <= tq, _dot_nt(qbd, k_new) - ct_new_rep, NEG)

    m = jnp.maximum(jnp.max(u_past, axis=-1, keepdims=True), jnp.max(u_new, axis=-1, keepdims=True)) + cq
    p_past = jnp.exp2((u_past + cq) - m)
    p_new = jnp.exp2((u_new + cq) - m)
    l = jnp.sum(p_past, axis=-1, keepdims=True) + jnp.sum(p_new, axis=-1, keepdims=True)
    acc = _dot_nt(p_past.astype(BF16), v_past) + _dot(p_new.astype(BF16), v_new)
    o = jnp.where(diag, acc / l, 0.0)
    out = o[0:t_new, :]
    for h in range(1, N_HEADS):
        out = out + o[h * t_new:(h + 1) * t_new, :]
    o_ref[...] = out

    @pl.when(n == n_seq - 1)
    def _():
        for s in range(1, PAGE_AHEAD + 1):
            for j in range(n_pages):
                for c in page_copies(n, lax.rem(n + s, PAGE_SLOTS), j):
                    c.wait()


def _attn_s(page_table, q, k, v, lf, cache_k, cache_v, cache_c, oa, layer, row0, t_new):
    n_seq, n_pages = page_table.shape
    blk0 = row0 // t_new
    new_spec = lambda w_: pl.BlockSpec((t_new, w_), lambda n, pt: (n, 0))
    in_specs = [new_spec(WIDTH_A), new_spec(WIDTH_A), new_spec(WIDTH_A), new_spec(LANES)]
    in_specs += [pl.BlockSpec(memory_space=pl.ANY)] * 4
    grid_spec = pltpu.PrefetchScalarGridSpec(
        num_scalar_prefetch=1,
        grid=(n_seq,),
        in_specs=in_specs,
        out_specs=pl.BlockSpec((t_new, WIDTH_A), lambda n, pt: (blk0 + n, 0)),
        scratch_shapes=[
            pltpu.VMEM((PAGE_SLOTS, n_pages, N_HEADS, HEAD_DIM, PAGE), F32),
            pltpu.VMEM((PAGE_SLOTS, n_pages, N_HEADS, HEAD_DIM, PAGE), F32),
            pltpu.VMEM((PAGE_SLOTS, n_pages, N_HEADS, PAGE), F32),
            pltpu.SemaphoreType.DMA((3, PAGE_SLOTS)),
        ],
    )
    return pl.pallas_call(
        functools.partial(_attn_s_kernel, n_pages=n_pages, t_new=t_new, layer=layer, n_seq=n_seq),
        grid_spec=grid_spec,
        out_shape=jax.ShapeDtypeStruct(oa.shape, oa.dtype),
        input_output_aliases={len(in_specs): 0},
        compiler_params=_params(("arbitrary",)),
        name="attn_s",
    )(page_table, q, k, v, lf, cache_k, cache_v, cache_c, oa)


def _lru_coeffs(xc, wa_ref, ba_ref, wx_ref, bx_ref, lam_ref):
    xb = xc.astype(BF16)
    sigmoid = lambda x: 0.5 * jnp.tanh(0.5 * x) + 0.5
    r = sigmoid(_dot(xb, wa_ref[...]) + ba_ref[...])
    ig = sigmoid(_dot(xb, wx_ref[...]) + bx_ref[...])
    log_a = (-LRU_C * r) * _softplus(-lam_ref[...])
    a = jnp.exp(log_a)
    y = 1.0 - a * a
    b = jnp.where(y > 0.0, y * lax.rsqrt(y), 0.0) * (ig * xc)
    return a, b


def _mix_p_kernel(z_ref, cw_ref, cb_ref, wa_ref, ba_ref, wx_ref, bx_ref, lam_ref, sw_ref,
                  ob_ref, oc_ref, h_ref, lc_ref, scc_ref,
                  ubuf, pbuf, a_s, b_s, hcar):
    tt = MIX_TILE
    hist = SUBLANES
    i = pl.program_id(1)

    @pl.when(i == 0)
    def _():
        ubuf[0:hist, :] = jnp.zeros((hist, WIDTH_B), F32)
        pbuf[0:hist, :] = jnp.zeros((hist, WIDTH_C), F32)
        hcar[...] = jnp.zeros(hcar.shape, F32)

    u = z_ref[:, 0:WIDTH_B]
    ubuf[hist:hist + tt, :] = u
    back = lambda ext, s: pltpu.roll(ext, s, 0)[hist:hist + tt, :]
    uext = ubuf[...]
    xc = cw_ref[0:1, :] * back(uext, 3)
    xc = xc + cw_ref[1:2, :] * back(uext, 2)
    xc = xc + cw_ref[2:3, :] * back(uext, 1)
    xc = xc + cw_ref[3:4, :] * u
    xc = xc + cb_ref[...]
    ubuf[0:hist, :] = ubuf[tt:tt + hist, :]
    lc_ref[...] = z_ref[tt - (CONV_B - 1):tt, 0:WIDTH_B]

    a, b = _lru_coeffs(xc, wa_ref, ba_ref, wx_ref, bx_ref, lam_ref)
    a_s[...] = a
    b_s[...] = b
    srow = lax.broadcasted_iota(jnp.int32, (SUBLANES, WIDTH_B), 0)

    def chunk(c, h):
        r0 = pl.multiple_of(c * SUBLANES, SUBLANES)
        aa = a_s[pl.ds(r0, SUBLANES), :]
        bb = b_s[pl.ds(r0, SUBLANES), :]
        for s in (1, 2, 4):
            keep = srow >= s
            a_sh = jnp.where(keep, pltpu.roll(aa, s, 0), 1.0)
            b_sh = jnp.where(keep, pltpu.roll(bb, s, 0), 0.0)
            bb = aa * b_sh + bb
            aa = aa * a_sh
        hc = aa * h + bb
        b_s[pl.ds(r0, SUBLANES), :] = hc
        return hc[SUBLANES - 1:SUBLANES, :]

    h_last = lax.fori_loop(0, tt // SUBLANES, chunk, hcar[...], unroll=4)
    hcar[...] = h_last
    h_ref[...] = h_last
    ob_ref[...] = (b_s[...] * jax.nn.gelu(z_ref[:, WIDTH_B:2 * WIDTH_B])).astype(BF16)

    o3 = 2 * WIDTH_B
    pc = z_ref[:, o3 + WIDTH_C:o3 + 2 * WIDTH_C] * z_ref[:, o3 + 2 * WIDTH_C:o3 + 3 * WIDTH_C]
    pbuf[hist:hist + tt, :] = pc
    pext = pbuf[...]
    uc = sw_ref[0:1, :] * back(pext, 2)
    uc = uc + sw_ref[1:2, :] * back(pext, 1)
    uc = uc + sw_ref[2:3, :] * pc
    pbuf[0:hist, :] = pbuf[tt:tt + hist, :]
    scc_ref[...] = pbuf[hist + tt - (CONV_C - 1):hist + tt, :]
    oc_ref[...] = (z_ref[:, o3:o3 + WIDTH_C] * uc).astype(BF16)


def _mix_weight_specs(layer):
    def im(*_):
        return (layer, 0, 0)

    return [
        pl.BlockSpec((None, CONV_B, WIDTH_B), im),
        pl.BlockSpec((None, 1, WIDTH_B), im),
        pl.BlockSpec((None, WIDTH_B, WIDTH_B), im),
        pl.BlockSpec((None, 1, WIDTH_B), im),
        pl.BlockSpec((None, WIDTH_B, WIDTH_B), im),
        pl.BlockSpec((None, 1, WIDTH_B), im),
        pl.BlockSpec((None, 1, WIDTH_B), im),
        pl.BlockSpec((None, CONV_C, WIDTH_C), im),
    ]


def _mix_p(zmix, weights, layer, batch, seq):
    tt = MIX_TILE
    nt = seq // tt
    rows = zmix.shape[0]
    return pl.pallas_call(
        _mix_p_kernel,
        grid=(batch, nt),
        in_specs=[pl.BlockSpec((tt, W_MIX), lambda b, i: (b * nt + i, 0))] + _mix_weight_specs(layer),
        out_specs=[
            pl.BlockSpec((tt, WIDTH_B), lambda b, i: (b * nt + i, 0)),
            pl.BlockSpec((tt, WIDTH_C), lambda b, i: (b * nt + i, 0)),
            pl.BlockSpec((None, 1, WIDTH_B), lambda b, i: (b, 0, 0)),
            pl.BlockSpec((None, CONV_B - 1, WIDTH_B), lambda b, i: (b, 0, 0)),
            pl.BlockSpec((None, CONV_C - 1, WIDTH_C), lambda b, i: (b, 0, 0)),
        ],
        out_shape=[
            jax.ShapeDtypeStruct((rows, WIDTH_B), BF16),
            jax.ShapeDtypeStruct((rows, WIDTH_C), BF16),
            jax.ShapeDtypeStruct((batch, 1, WIDTH_B), F32),
            jax.ShapeDtypeStruct((batch, CONV_B - 1, WIDTH_B), F32),
            jax.ShapeDtypeStruct((batch, CONV_C - 1, WIDTH_C), F32),
        ],
        scratch_shapes=[
            pltpu.VMEM((tt + SUBLANES, WIDTH_B), F32),
            pltpu.VMEM((tt + SUBLANES, WIDTH_C), F32),
            pltpu.VMEM((tt, WIDTH_B), F32),
            pltpu.VMEM((tt, WIDTH_B), F32),
            pltpu.VMEM((1, WIDTH_B), F32),
        ],
        compiler_params=_params(("arbitrary", "arbitrary")),
        name="mix_p",
    )(zmix, *weights)


def _mix_s_kernel(z_ref, h0_ref, lch_ref, sch_ref, cw_ref, cb_ref, wa_ref, ba_ref, wx_ref, bx_ref,
                  lam_ref, sw_ref, ob_alias, oc_alias, ob_ref, oc_ref, hs_ref, pc_ref, *, t_new):
    tt = z_ref.shape[0]
    o3 = 2 * WIDTH_B
    tpos = lax.broadcasted_iota(jnp.int32, (tt, WIDTH_B), 0) % t_new

    def back(x, hist, s):
        return jnp.where(tpos >= s, pltpu.roll(x, s, 0), pltpu.roll(hist, tt - (t_new - s), 0))

    u = z_ref[:, 0:WIDTH_B]
    lch = lch_ref[...]
    xc = cw_ref[0:1, :] * back(u, lch, 3)
    xc = xc + cw_ref[1:2, :] * back(u, lch, 2)
    xc = xc + cw_ref[2:3, :] * back(u, lch, 1)
    xc = xc + cw_ref[3:4, :] * u
    xc = xc + cb_ref[...]
    a, b = _lru_coeffs(xc, wa_ref, ba_ref, wx_ref, bx_ref, lam_ref)
    b = b + a * h0_ref[...]
    for s in (1, 2, 4):
        keep = tpos >= s
        a_sh = jnp.where(keep, pltpu.roll(a, s, 0), 1.0)
        b_sh = jnp.where(keep, pltpu.roll(b, s, 0), 0.0)
        b = a * b_sh + b
        a = a * a_sh
    hs_ref[...] = b
    ob_ref[...] = (b * jax.nn.gelu(z_ref[:, WIDTH_B:2 * WIDTH_B])).astype(BF16)

    pc = z_ref[:, o3 + WIDTH_C:o3 + 2 * WIDTH_C] * z_ref[:, o3 + 2 * WIDTH_C:o3 + 3 * WIDTH_C]
    sch = sch_ref[...]
    uc = sw_ref[0:1, :] * back(pc, sch, 2)
    uc = uc + sw_ref[1:2, :] * back(pc, sch, 1)
    uc = uc + sw_ref[2:3, :] * pc
    pc_ref[...] = pc
    oc_ref[...] = (z_ref[:, o3:o3 + WIDTH_C] * uc).astype(BF16)


def _mix_s(zmix, h0e, lch, sch, weights, ob, oc, layer, row0, n_seq, t_new):
    ns = SEQ_PER_MIX_STEP
    tt = ns * t_new
    blk0 = row0 // tt
    rows = n_seq * t_new
    row_spec = lambda w_: pl.BlockSpec((tt, w_), lambda i: (i, 0))
    all_spec = lambda w_: pl.BlockSpec((tt, w_), lambda i: (blk0 + i, 0))
    in_specs = [all_spec(W_MIX), row_spec(WIDTH_B), row_spec(WIDTH_B), row_spec(WIDTH_C)]
    in_specs += _mix_weight_specs(layer)
    in_specs += [pl.BlockSpec(memory_space=pl.ANY)] * 2
    return pl.pallas_call(
        functools.partial(_mix_s_kernel, t_new=t_new),
        grid=(n_seq // ns,),
        in_specs=in_specs,
        out_specs=[all_spec(WIDTH_B), all_spec(WIDTH_C), row_spec(WIDTH_B), row_spec(WIDTH_C)],
        out_shape=[
            jax.ShapeDtypeStruct(ob.shape, ob.dtype),
            jax.ShapeDtypeStruct(oc.shape, oc.dtype),
            jax.ShapeDtypeStruct((rows, WIDTH_B), F32),
            jax.ShapeDtypeStruct((rows, WIDTH_C), F32),
        ],
        input_output_aliases={len(in_specs) - 2: 0, len(in_specs) - 1: 1},
        compiler_params=_params(("parallel",)),
        name="mix_s",
    )(zmix, h0e, lch, sch, *weights, ob, oc)


def _merge_kernel(x_ref, oa_ref, ob_ref, oc_ref, g_ref, wb_ref, wo_ref, o_ref):
    merged = None
    lo = 0
    for b, (o_b, width) in enumerate(((oa_ref, WIDTH_A), (ob_ref, WIDTH_B), (oc_ref, WIDTH_C))):
        proj = _dot(o_b[...].astype(BF16), wb_ref[lo:lo + width, :])
        term = g_ref[:, b * D_MODEL:(b + 1) * D_MODEL].astype(F32) * proj
        merged = term if merged is None else merged + term
        lo += width
    o_ref[...] = x_ref[...] + _dot(merged.astype(BF16), wo_ref[...])


def _merge(x, oa, ob, oc, g, wb, wo, layer):
    rows = x.shape[0]
    row_spec = lambda w_: pl.BlockSpec((ROW_TILE, w_), lambda i: (i, 0))
    return pl.pallas_call(
        _merge_kernel,
        grid=(rows // ROW_TILE,),
        in_specs=[
            row_spec(D_MODEL), row_spec(WIDTH_A), row_spec(WIDTH_B), row_spec(WIDTH_C), row_spec(W_G),
            _resident((None, WIDTH_A + WIDTH_B + WIDTH_C, D_MODEL), lambda i: (layer, 0, 0)),
            _resident((None, D_MODEL, D_MODEL), lambda i: (layer, 0, 0)),
        ],
        out_specs=row_spec(D_MODEL),
        out_shape=jax.ShapeDtypeStruct((rows, D_MODEL), F32),
        compiler_params=_params(("parallel",)),
        name="merge",
    )(x, oa, ob, oc, g, wb, wo)


def kernel(x_prompt, x_sample, cache_k, cache_v, cache_logf, state_lru_h, state_lru_conv, state_sc_conv,
           page_table, ffn1_norm, ffn1_w_in, ffn1_w_out, mix_norm, w_in, b_forget, q_norm, k_norm,
           lru_conv_w, lru_conv_b, lru_wa, lru_ba, lru_wx, lru_bx, lru_lambda, sc_conv_w, w_branch, w_out,
           ffn2_norm, ffn2_w_in, ffn2_w_out):
    batch, seq, _ = x_prompt.shape
    n_seq, t_new, _ = x_sample.shape
    depth = w_in.shape[0]
    rows_p = batch * seq
    rows_s = n_seq * t_new

    vec = lambda a: a[:, None, :]
    w1a, w2a = _to_bf16(ffn1_w_in), _to_bf16(ffn1_w_out)
    w1b, w2b = _to_bf16(ffn2_w_in), _to_bf16(ffn2_w_out)
    w_in_p = (w_in[:, :, :OFF_F].astype(BF16), w_in[:, :, OFF_MIX:].astype(BF16),
              jnp.pad(w_in[:, :, OFF_F:OFF_MIX], ((0, 0), (0, 0), (0, LANES - N_HEADS))).astype(BF16))
    bf_p = jnp.pad(b_forget, ((0, 0), (0, LANES - N_HEADS)))[:, None, :]
    qn_p = vec(jnp.tile(q_norm, (1, N_HEADS)))
    kn_p = vec(jnp.tile(k_norm, (1, N_HEADS)))
    eye = jnp.eye(N_BLOCKS_B, dtype=lru_wa.dtype)
    dense = lambda w: jnp.einsum('lgij,gh->lgihj', w, eye).reshape(depth, WIDTH_B, WIDTH_B).astype(BF16)
    mix_weights = (lru_conv_w, vec(lru_conv_b), dense(lru_wa), vec(lru_ba), dense(lru_wx), vec(lru_bx),
                   vec(lru_lambda), sc_conv_w)
    wb_b, wo_b = _to_bf16(w_branch), _to_bf16(w_out)
    ck4 = jnp.transpose(cache_k, (0, 1, 3, 4, 2))
    cv4 = jnp.transpose(cache_v, (0, 1, 3, 4, 2))
    clft = jnp.swapaxes(cache_logf, 2, 3)
    cpool = _pool_prefix(clft.reshape(-1, PAGE)).reshape(clft.shape)

    def seq_rows(state, at_end):
        r = state.shape[1]
        pad = (t_new - r, 0) if at_end else (0, t_new - r)
        return jnp.pad(state, ((0, 0), pad, (0, 0))).reshape(rows_s, state.shape[-1])

    per_seq = lambda a: a.reshape(n_seq, t_new, a.shape[-1])
    x = (x_prompt.reshape(rows_p, D_MODEL), x_sample.reshape(rows_s, D_MODEL))
    prev = None
    outs_p = [[] for _ in range(3)]
    outs_s = [[] for _ in range(6)]
    for l in range(depth):
        x = _ffn(x, vec(ffn1_norm), w1a, w2a, l, rows_p, rows_s, False)
        (qb, kb, vb, qs, ks, vs, lfs, kt, vt, lft, zmix, gates) = _inproj(
            x, vec(mix_norm), w_in_p, bf_p, qn_p, kn_p, l, depth, batch, seq, rows_s, prev)
        prev = (kt, vt, lft)
        ct = _cumsum(lft, l, batch, seq)
        oa = _attn_p(qb, kb, vb, ct.T, ct, batch, seq)
        oa = _attn_s(page_table, qs, ks, vs, lfs, ck4, cv4, cpool, oa, l, rows_p, t_new)
        ob, oc, h_p, lc_p, sc_p = _mix_p(zmix, mix_weights, l, batch, seq)
        ob, oc, hs_s, pc_s = _mix_s(zmix, seq_rows(state_lru_h[:, l][:, None, :], False),
                                    seq_rows(state_lru_conv[:, l], True),
                                    seq_rows(state_sc_conv[:, l], True), mix_weights, ob, oc,
                                    l, rows_p, n_seq, t_new)
        x = _merge(x, oa, ob, oc, gates, wb_b, wo_b, l)
        x = _ffn(x, vec(ffn2_norm), w1b, w2b, l, rows_p, rows_s, l == depth - 1)
        for dst, val in zip(outs_p, (h_p[:, 0], lc_p, sc_p)):
            dst.append(val)
        for dst, val in zip(outs_s, (ks.reshape(n_seq, t_new, N_HEADS, HEAD_DIM),
                                     vs.reshape(n_seq, t_new, N_HEADS, HEAD_DIM),
                                     per_seq(lfs[:, :N_HEADS]),
                                     per_seq(hs_s)[:, -1],
                                     per_seq(zmix[rows_p:, :WIDTH_B])[:, t_new - (CONV_B - 1):],
                                     per_seq(pc_s)[:, t_new - (CONV_C - 1):])):
            dst.append(val)
    stack = lambda xs: jnp.stack(xs, axis=1)
    kt, vt, lft = prev
    heads = lambda a: jnp.transpose(a.reshape(batch, depth, N_HEADS, HEAD_DIM, seq), (0, 1, 4, 2, 3))
    y_p, y_s = x
    return ((y_p.reshape(batch, seq, D_MODEL), y_s.reshape(n_seq, t_new, D_MODEL),
             heads(kt), heads(vt), jnp.swapaxes(lft, 2, 3))
            + tuple(stack(o) for o in outs_p) + tuple(stack(o) for o in outs_s))
```
